```python
import math
import jax, jax.numpy as jnp
from jax import lax
import numpy as np

D_MODEL = 1024
BATCH = 4
SEQ = 8192
DEPTH = 2
DEC_BATCH = 2
DEC_SEQ = 16384
PAST_LEN = 128

HEAD_DIM = 64
MIX_W = D_MODEL
A_HEADS = 4
A_W = A_HEADS * HEAD_DIM
LORA = 64
B_HEADS = 4
B_W = B_HEADS * HEAD_DIM
DILATED_PAIRS = ((128, 1), (512, 4), (2048, 16))
C_HEADS = 8
C_KV_HEADS = 2
C_QW = C_HEADS * HEAD_DIM
C_KVW = C_KV_HEADS * HEAD_DIM
C_RADIUS = 128
ROPE_THETA = 10000.0
RMS_EPS = 1e-5
HEADNORM_EPS = 64e-5
NEG_INF = -1e30

SHIFT_SPLITS = (A_W, A_W, A_W, LORA, LORA, LORA, LORA)
TSHIFT_W = sum(SHIFT_SPLITS)
REST_SPLITS = (A_W, B_W, B_W, B_W, B_W, C_QW, C_KVW, C_KVW, C_QW)
IN_W = TSHIFT_W + sum(REST_SPLITS)

kernel_name = "hybrid_bidir_rwkv7_dilated_swa_encoder"


def _split(t, sizes):
    idx = [int(i) for i in np.cumsum(sizes)[:-1]]
    return jnp.split(t, idx, axis=-1)


def rms_norm(x, g):
    xf = x.astype(jnp.float32)
    y = xf * lax.rsqrt(jnp.mean(xf * xf, axis=-1, keepdims=True) + RMS_EPS)
    return (y * g.astype(jnp.float32)).astype(x.dtype)


def rope(t):
    T, hd = t.shape[1], t.shape[-1]
    inv = ROPE_THETA ** (-jnp.arange(0, hd, 2, dtype=jnp.float32) / hd)
    ang = jnp.arange(T, dtype=jnp.float32)[:, None] * inv[None, :]
    cos = jnp.cos(ang)[None, :, None, :]
    sin = jnp.sin(ang)[None, :, None, :]
    tf = t.astype(jnp.float32)
    t1, t2 = tf[..., : hd // 2], tf[..., hd // 2:]
    return jnp.concatenate([t1 * cos - t2 * sin, t2 * cos + t1 * sin], axis=-1).astype(t.dtype)


def centred_token_shift(f, mu):
    zero = jnp.zeros_like(f[:, :1])
    prev = jnp.concatenate([zero, f[:, :-1]], axis=1)
    nxt = jnp.concatenate([f[:, 1:], zero], axis=1)
    return f + mu * (0.5 * (prev + nxt) - f)


def banded_attention(q, k, v, radius, sink=None):
    b, L, hq, hd = q.shape
    hkv = k.shape[2]
    grp = hq // hkv
    blk = radius
    nb = -(-L // blk)
    lp = nb * blk
    q = jnp.pad(q, ((0, 0), (0, lp - L), (0, 0), (0, 0)))
    kv_pad = ((0, 0), (blk, lp - L + blk), (0, 0), (0, 0))
    k = jnp.pad(k, kv_pad).reshape(b, nb + 2, blk, hkv, hd)
    v = jnp.pad(v, kv_pad).reshape(b, nb + 2, blk, hkv, hd)
    kb = jnp.concatenate([k[:, :-2], k[:, 1:-1], k[:, 2:]], axis=2)
    vb = jnp.concatenate([v[:, :-2], v[:, 1:-1], v[:, 2:]], axis=2)
    qb = q.reshape(b, nb, blk, hkv, grp, hd)
    s = jnp.einsum('bnqhgd,bnkhd->bnhgqk', qb, kb,
                   preferred_element_type=jnp.float32) * (hd ** -0.5)
    qpos = jnp.arange(lp).reshape(nb, blk)
    kpos = (jnp.arange(nb)[:, None] - 1) * blk + jnp.arange(3 * blk)[None, :]
    kp = kpos[:, None, :]
    mask = (jnp.abs(kp - qpos[:, :, None]) <= radius) & (kp >= 0) & (kp < L)
    s = jnp.where(mask[None, :, None, None], s, NEG_INF)
    m = jnp.max(s, axis=-1, keepdims=True)
    if sink is not None:
        sk = sink.astype(jnp.float32).reshape(hkv, grp)[None, None, :, :, None, None]
        m = jnp.maximum(m, sk)
    p = jnp.exp(s - m)
    denom = jnp.sum(p, axis=-1, keepdims=True)
    if sink is not None:
        denom = denom + jnp.exp(sk - m)
    o = jnp.einsum('bnhgqk,bnkhd->bnhgqd', p, vb.astype(jnp.float32)) / denom
    o = o.transpose(0, 1, 4, 2, 3, 5).reshape(b, lp, hq, hd)[:, :L]
    lse = (m + jnp.log(denom))[..., 0].transpose(0, 1, 4, 2, 3).reshape(b, lp, hq)[:, :L]
    return o, lse


def dilated_mixture(q, k, v):
    b, T, h, hd = q.shape
    outs, lses = [], []
    for window, dil in DILATED_PAIRS:
        radius = window // (2 * dil)
        L = T // dil

        def to_strided(t):
            return t.reshape(b, L, dil, h, hd).transpose(0, 2, 1, 3, 4).reshape(b * dil, L, h, hd)

        o, lse = banded_attention(to_strided(q), to_strided(k), to_strided(v), radius)
        outs.append(o.reshape(b, dil, L, h, hd).transpose(0, 2, 1, 3, 4).reshape(b, T, h, hd))
        lses.append(lse.reshape(b, dil, L, h).transpose(0, 2, 1, 3).reshape(b, T, h))
    wts = jax.nn.softmax(jnp.stack(lses, axis=0), axis=0)
    return jnp.sum(wts[..., None] * jnp.stack(outs, axis=0), axis=0)


def wkv7_scan(r, w, k, v, kk, a, reverse):
    b, T, h, n = r.shape

    def step(S, inp):
        r_t, w_t, k_t, v_t, kk_t, a_t = inp
        sa = jnp.einsum('bhij,bhj->bhi', S, -kk_t)
        S = (S * w_t[:, :, None, :] + sa[..., :, None] * (kk_t * a_t)[..., None, :]
             + v_t[..., :, None] * k_t[..., None, :])
        return S, jnp.einsum('bhij,bhj->bhi', S, r_t)

    xs = tuple(jnp.moveaxis(z, 1, 0) for z in (r, w, k, v, kk, a))
    S0 = jnp.zeros((b, h, n, n), jnp.float32)
    _, y = lax.scan(step, S0, xs, reverse=reverse)
    return jnp.moveaxis(y, 0, 1)


def rwkv7_branch(r, k, v, wd, ad, w0, w2, a0, a2, k_k, k_a, r_k, ln_w, ln_b):
    f32 = jnp.float32
    b, t, _ = r.shape
    r, k, v = r.astype(f32), k.astype(f32), v.astype(f32)
    w = -jax.nn.softplus(-(w0 + jnp.einsum('btel,elc->btec', jnp.tanh(wd.astype(f32)), w2.astype(f32)))) - 0.5
    decay = jnp.exp(-jnp.exp(w))
    a = jax.nn.sigmoid(a0 + jnp.einsum('btel,elc->btec', ad.astype(f32), a2.astype(f32)))

    def heads(z):
        return z.reshape(b, t, A_HEADS, HEAD_DIM)

    kk = heads(k * k_k)
    kk = kk / jnp.maximum(jnp.sqrt(jnp.sum(kk * kk, axis=-1, keepdims=True)), 1e-12)
    k_eff = k[:, :, None, :] * (1.0 + (a - 1.0) * k_a)
    rh, vh = heads(r), heads(v)
    y_f = wkv7_scan(rh, heads(decay[:, :, 0]), heads(k_eff[:, :, 0]), vh, kk, heads(a[:, :, 0]), reverse=False)
    y_b = wkv7_scan(rh, heads(decay[:, :, 1]), heads(k_eff[:, :, 1]), vh, kk, heads(a[:, :, 1]), reverse=True)
    y = y_f + y_b
    mu = jnp.mean(y, axis=-1, keepdims=True)
    var = jnp.mean(jnp.square(y - mu), axis=-1, keepdims=True)
    yn = (y - mu) * lax.rsqrt(var + HEADNORM_EPS)
    yn = yn * ln_w.reshape(A_HEADS, HEAD_DIM) + ln_b.reshape(A_HEADS, HEAD_DIM)
    bonus = jnp.sum(rh * heads(k_eff[:, :, 0] + k_eff[:, :, 1]) * r_k, axis=-1, keepdims=True) * vh
    return (yn + bonus).reshape(b, t, A_W)


def hybrid_layer(x, norm_g, w_in, tshift_mu, rwkv_w0, rwkv_w2, rwkv_a0, rwkv_a2, rwkv_k_k, rwkv_k_a,
                 rwkv_r_k, ln_x_w, ln_x_b, attn_sink, w_out):
    b, t, _ = x.shape
    h = rms_norm(x, norm_g)
    proj = jnp.einsum('btd,de->bte', h, w_in)
    shifted = centred_token_shift(proj[..., :TSHIFT_W], tshift_mu)
    a_r, a_k, a_v, wd_f, wd_b, ad_f, ad_b = _split(shifted, SHIFT_SPLITS)
    a_g, b_q, b_k, b_v, b_g, c_q, c_k, c_v, c_g = _split(proj[..., TSHIFT_W:], REST_SPLITS)

    y_a = rwkv7_branch(a_r, a_k, a_v, jnp.stack([wd_f, wd_b], axis=2), jnp.stack([ad_f, ad_b], axis=2),
                       rwkv_w0, rwkv_w2, rwkv_a0, rwkv_a2, rwkv_k_k, rwkv_k_a, rwkv_r_k, ln_x_w, ln_x_b)
    bh = lambda z: z.reshape(b, t, B_HEADS, HEAD_DIM)
    y_b = dilated_mixture(rope(bh(b_q)), rope(bh(b_k)), bh(b_v)).reshape(b, t, B_W)
    y_c, _ = banded_attention(rope(c_q.reshape(b, t, C_HEADS, HEAD_DIM)),
                              rope(c_k.reshape(b, t, C_KV_HEADS, HEAD_DIM)),
                              c_v.reshape(b, t, C_KV_HEADS, HEAD_DIM), C_RADIUS, sink=attn_sink)
    y_c = y_c.reshape(b, t, C_QW)

    mix = jnp.concatenate([y_a.astype(x.dtype) * jax.nn.silu(a_g),
                           y_b.astype(x.dtype) * jax.nn.silu(b_g),
                           y_c.astype(x.dtype) * jax.nn.silu(c_g)], axis=-1)
    return x + jnp.einsum('btc,cd->btd', mix, w_out)


def trunk(x, norm_g, w_in, tshift_mu, rwkv_w0, rwkv_w2, rwkv_a0, rwkv_a2, rwkv_k_k, rwkv_k_a,
          rwkv_r_k, ln_x_w, ln_x_b, attn_sink, w_out, final_g):
    for l in range(DEPTH):
        x = hybrid_layer(x, norm_g[l], w_in[l], tshift_mu[l], rwkv_w0[l], rwkv_w2[l], rwkv_a0[l], rwkv_a2[l],
                         rwkv_k_k[l], rwkv_k_a[l], rwkv_r_k[l], ln_x_w[l], ln_x_b[l], attn_sink[l], w_out[l])
    return rms_norm(x, final_g)


def setup_inputs(seed: int = 0) -> dict:
    key = jax.random.key(seed)
    ks = jax.random.split(key, 20)
    nrm = jax.random.normal
    f32 = jnp.float32
    return {
        "x_prompt": nrm(ks[0], (BATCH, SEQ, D_MODEL), f32),
        "x_sample": nrm(ks[1], (DEC_BATCH, DEC_SEQ, D_MODEL), f32),
        "norm_g": 1.0 + 0.1 * nrm(ks[2], (DEPTH, D_MODEL), f32),
        "w_in": nrm(ks[3], (DEPTH, D_MODEL, IN_W), f32) * D_MODEL ** -0.5,
        "tshift_mu": jax.random.uniform(ks[4], (DEPTH, TSHIFT_W), f32),
        "rwkv_w0": -2.0 + nrm(ks[5], (DEPTH, 2, A_W), f32),
        "rwkv_w2": 0.1 * nrm(ks[6], (DEPTH, 2, LORA, A_W), f32),
        "rwkv_a0": 0.5 * nrm(ks[7], (DEPTH, 2, A_W), f32),
        "rwkv_a2": 0.1 * nrm(ks[8], (DEPTH, 2, LORA, A_W), f32),
        "rwkv_k_k": 0.85 + 0.05 * nrm(ks[9], (DEPTH, A_W), f32),
        "rwkv_k_a": 1.0 + 0.05 * nrm(ks[10], (DEPTH, A_W), f32),
        "rwkv_r_k": 0.1 * nrm(ks[11], (DEPTH, A_HEADS, HEAD_DIM), f32),
        "ln_x_w": 1.0 + 0.1 * nrm(ks[12], (DEPTH, A_W), f32),
        "ln_x_b": 0.01 * nrm(ks[13], (DEPTH, A_W), f32),
        "attn_sink": 0.5 * nrm(ks[14], (DEPTH, C_HEADS), f32),
        "w_out": nrm(ks[15], (DEPTH, MIX_W, D_MODEL), f32) * MIX_W ** -0.5,
        "final_g": 1.0 + 0.1 * nrm(ks[16], (D_MODEL,), f32),
    }


def reference(x_prompt, x_sample, norm_g, w_in, tshift_mu, rwkv_w0, rwkv_w2, rwkv_a0, rwkv_a2, rwkv_k_k,
              rwkv_k_a, rwkv_r_k, ln_x_w, ln_x_b, attn_sink, w_out, final_g):
    y_prompt = trunk(x_prompt, norm_g, w_in, tshift_mu, rwkv_w0, rwkv_w2, rwkv_a0, rwkv_a2, rwkv_k_k,
                     rwkv_k_a, rwkv_r_k, ln_x_w, ln_x_b, attn_sink, w_out, final_g)
    y_sample = trunk(x_sample, norm_g, w_in, tshift_mu, rwkv_w0, rwkv_w2, rwkv_a0, rwkv_a2, rwkv_k_k,
                     rwkv_k_a, rwkv_r_k, ln_x_w, ln_x_b, attn_sink, w_out, final_g)
    return (y_prompt, y_sample)
```

```python
import functools
import math

import jax
import jax.numpy as jnp
import numpy as np
from jax import lax
from jax.experimental import pallas as pl
from jax.experimental.pallas import tpu as pltpu

F32 = jnp.float32
BF16 = jnp.bfloat16
HIGHEST = lax.Precision.HIGHEST

D_MODEL = 1024
HEAD_DIM = 64
A_HEADS = 4
A_W = A_HEADS * HEAD_DIM
LORA = 64
B_HEADS = 4
B_W = B_HEADS * HEAD_DIM
DILATED_PAIRS = ((128, 1), (512, 4), (2048, 16))
C_HEADS = 8
C_KV_HEADS = 2
C_QW = C_HEADS * HEAD_DIM
C_KVW = C_KV_HEADS * HEAD_DIM
C_RADIUS = 128
ROPE_THETA = 10000.0
RMS_EPS = 1e-5
HEADNORM_EPS = 64e-5
NEG_INF = -1e30
TSHIFT_W = 3 * A_W + 4 * LORA
IN_W = 3584

COL_AG = TSHIFT_W
COL_BQ = COL_AG + A_W
COL_BK = COL_BQ + B_W
COL_BV = COL_BK + B_W
COL_BG = COL_BV + B_W
COL_CQ = COL_BG + B_W
COL_CK = COL_CQ + C_QW
COL_CV = COL_CK + C_KVW
COL_CG = COL_CV + C_KVW

LANES = 128
ROW_TILE = 512
WKV_CHUNK = 64
WKV_ROWS = 256
HALO_ROWS = 8
VMEM_LIMIT = 56 * 1024 * 1024


def _dot(a, b, dims, precision=None):
    return lax.dot_general(a, b, (dims, ((), ())), precision=precision, preferred_element_type=F32)


def _mm(a, b, precision=None):
    return _dot(a, b, ((1,), (0,)), precision)


def _mm_nt(a, b, precision=None):
    return _dot(a, b, ((1,), (1,)), precision)


def _mm_tn(a, b, precision=None):
    return _dot(a, b, ((0,), (0,)), precision)


def _sigmoid(x):
    return 1.0 / (1.0 + jnp.exp(-x))


def _silu(x):
    return x * _sigmoid(x)


def _softplus(x):
    return jnp.maximum(x, 0.0) + jnp.log(1.0 + jnp.exp(-jnp.abs(x)))


def _proj_kernel(x_ref, g_ref, w_ref, cos_ref, sin_ref,
                 shift_ref, ag_ref, bg_ref, cq_ref, ck_ref, cv_ref, cg_ref,
                 bq1_ref, bk1_ref, bv1_ref, bq4_ref, bk4_ref, bv4_ref, bq16_ref, bk16_ref, bv16_ref,
                 tmp_ref):
    tm = x_ref.shape[1]
    x = x_ref[0]
    h = x * lax.rsqrt(jnp.mean(x * x, axis=-1, keepdims=True) + RMS_EPS) * g_ref[...]
    hb = h.astype(BF16)

    def proj(c0, c1):
        return _mm(hb, w_ref[:, c0:c1])

    cos = cos_ref[...]
    sin = sin_ref[...]
    lane = lax.broadcasted_iota(jnp.int32, (tm, LANES), 1)
    first_half = (lane % HEAD_DIM) < (HEAD_DIM // 2)

    def rope(t):
        outs = []
        for j in range(t.shape[1] // LANES):
            tj = t[:, j * LANES:(j + 1) * LANES]
            partner = jnp.where(first_half,
                                pltpu.roll(tj, LANES - HEAD_DIM // 2, axis=1),
                                pltpu.roll(tj, HEAD_DIM // 2, axis=1))
            outs.append(tj * cos + partner * sin)
        return jnp.concatenate(outs, axis=1) if len(outs) > 1 else outs[0]

    def deinterleave(val, ref1, ref4, ref16):
        ref1[0] = val.astype(BF16)
        nl = B_W // LANES
        for j in range(nl):
            tmp_ref[j] = val[:, j * LANES:(j + 1) * LANES]
        for d, ref in ((4, ref4), (16, ref16)):
            for r in range(d):
                parts = [tmp_ref[j, pl.ds(r, tm // d, stride=d), :] for j in range(nl)]
                ref[0, r] = jnp.concatenate(parts, axis=1).astype(BF16)

    for c0 in range(0, TSHIFT_W, 256):
        shift_ref[0, :, c0:c0 + 256] = proj(c0, c0 + 256)
    ag_ref[0] = _silu(proj(COL_AG, COL_AG + A_W))
    bg_ref[0] = _silu(proj(COL_BG, COL_BG + B_W))
    cg_ref[0] = _silu(proj(COL_CG, COL_CG + C_QW))
    scale = HEAD_DIM ** -0.5
    deinterleave(rope(proj(COL_BQ, COL_BQ + B_W)) * scale, bq1_ref, bq4_ref, bq16_ref)
    deinterleave(rope(proj(COL_BK, COL_BK + B_W)), bk1_ref, bk4_ref, bk16_ref)
    deinterleave(proj(COL_BV, COL_BV + B_W), bv1_ref, bv4_ref, bv16_ref)
    for c0 in range(0, C_QW, 256):
        cq_ref[0, :, c0:c0 + 256] = (rope(proj(COL_CQ + c0, COL_CQ + c0 + 256)) * scale).astype(BF16)
    ck_ref[0] = rope(proj(COL_CK, COL_CK + C_KVW)).astype(BF16)
    cv_ref[0] = proj(COL_CV, COL_CV + C_KVW).astype(BF16)


def _proj_call(x, g, w_bf16, cos_t, sin_t):
    b, t, _ = x.shape
    tm = ROW_TILE
    grid = (b, t // tm)

    def row(w):
        return pl.BlockSpec((1, tm, w), lambda bi, i: (bi, i, 0))

    def strided(d):
        return pl.BlockSpec((1, d, tm // d, B_W), lambda bi, i: (bi, 0, i, 0))

    def nat(w, dt):
        return jax.ShapeDtypeStruct((b, t, w), dt)

    def sshape(d):
        return jax.ShapeDtypeStruct((b, d, t // d, B_W), BF16)

    out_shape = [nat(TSHIFT_W, F32), nat(A_W, F32), nat(B_W, F32),
                 nat(C_QW, BF16), nat(C_KVW, BF16), nat(C_KVW, BF16), nat(C_QW, F32),
                 nat(B_W, BF16), nat(B_W, BF16), nat(B_W, BF16),
                 sshape(4), sshape(4), sshape(4), sshape(16), sshape(16), sshape(16)]
    out_specs = [row(TSHIFT_W), row(A_W), row(B_W),
                 row(C_QW), row(C_KVW), row(C_KVW), row(C_QW),
                 row(B_W), row(B_W), row(B_W),
                 strided(4), strided(4), strided(4), strided(16), strided(16), strided(16)]
    in_specs = [row(D_MODEL),
                pl.BlockSpec((1, D_MODEL), lambda bi, i: (0, 0)),
                pl.BlockSpec((D_MODEL, IN_W), lambda bi, i: (0, 0)),
                pl.BlockSpec((tm, LANES), lambda bi, i: (i, 0)),
                pl.BlockSpec((tm, LANES), lambda bi, i: (i, 0))]
    return pl.pallas_call(
        _proj_kernel, grid=grid, in_specs=in_specs, out_specs=out_specs, out_shape=out_shape,
        scratch_shapes=[pltpu.VMEM((B_W // LANES, tm, LANES), F32)],
        compiler_params=pltpu.CompilerParams(
            dimension_semantics=("arbitrary", "arbitrary"), vmem_limit_bytes=VMEM_LIMIT),
        name="proj",
    )(x, g, w_bf16, cos_t, sin_t)


def _prep_kernel(f_ref, fp_ref, fn_ref, mu_ref, w0_ref, w2_ref, a0_ref, a2_ref, kk_ref_, ka_ref, rk_ref, gsum_ref,
                 r_ref, v_ref, kkn_ref, bonus_ref, logw_ref, keff_ref, beta_ref):
    i = pl.program_id(1)
    n = pl.num_programs(1)
    tp = f_ref.shape[1]
    f = f_ref[0]
    row = lax.broadcasted_iota(jnp.int32, (tp, 1), 0)
    prev_row = jnp.where(i > 0, fp_ref[0, HALO_ROWS - 1:HALO_ROWS, :], 0.0)
    next_row = jnp.where(i < n - 1, fn_ref[0, 0:1, :], 0.0)
    prev = jnp.where(row == 0, prev_row, pltpu.roll(f, 1, axis=0))
    nxt = jnp.where(row == tp - 1, next_row, pltpu.roll(f, tp - 1, axis=0))
    s = f + mu_ref[...] * (0.5 * (prev + nxt) - f)

    r = s[:, 0:A_W]
    k = s[:, A_W:2 * A_W]
    v = s[:, 2 * A_W:3 * A_W]
    lora_in = s[:, 3 * A_W:TSHIFT_W]
    wl = _mm(jnp.tanh(lora_in), w2_ref[...], HIGHEST)
    al = _mm(lora_in, a2_ref[...], HIGHEST)
    gsum = gsum_ref[...]

    kk0 = k * kk_ref_[...]
    ss = _mm(kk0 * kk0, gsum, HIGHEST)
    kkn = kk0 / jnp.maximum(jnp.sqrt(ss), 1e-12)
    r_ref[0] = r
    v_ref[0] = v
    kkn_ref[0] = kkn
    keff_sum = jnp.zeros_like(k)
    for e in range(2):
        w = -_softplus(-(w0_ref[e:e + 1, :] + wl[:, e * A_W:(e + 1) * A_W])) - 0.5
        logw_ref[e, 0] = -jnp.exp(w)
        a = _sigmoid(a0_ref[e:e + 1, :] + al[:, e * A_W:(e + 1) * A_W])
        keff = k * (1.0 + (a - 1.0) * ka_ref[...])
        keff_ref[e, 0] = keff
        beta_ref[e, 0] = a * kkn
        keff_sum = keff_sum + keff
    bonus_ref[0] = _mm(r * keff_sum * rk_ref[...], gsum, HIGHEST) * v


def _prep_call(shift, mu, w0, w2p, a0, a2p, k_k, k_a, r_k, gsum):
    b, t, _ = shift.shape
    tp = ROW_TILE
    nb = tp // HALO_ROWS
    last = t // HALO_ROWS - 1
    grid = (b, t // tp)

    def const(shape):
        return pl.BlockSpec(shape, lambda bi, i: (0,) * len(shape))

    in_specs = [pl.BlockSpec((1, tp, TSHIFT_W), lambda bi, i: (bi, i, 0)),
                pl.BlockSpec((1, HALO_ROWS, TSHIFT_W), lambda bi, i: (bi, jnp.maximum(i * nb - 1, 0), 0)),
                pl.BlockSpec((1, HALO_ROWS, TSHIFT_W), lambda bi, i: (bi, jnp.minimum((i + 1) * nb, last), 0)),
                const((1, TSHIFT_W)), const((2, A_W)), const((4 * LORA, 2 * A_W)),
                const((2, A_W)), const((4 * LORA, 2 * A_W)),
                const((1, A_W)), const((1, A_W)), const((1, A_W)), const((A_W, A_W))]
    tok = pl.BlockSpec((1, tp, A_W), lambda bi, i: (bi, i, 0))
    both = pl.BlockSpec((2, 1, tp, A_W), lambda bi, i: (0, bi, i, 0))
    tok_shape = jax.ShapeDtypeStruct((b, t, A_W), F32)
    both_shape = jax.ShapeDtypeStruct((2, b, t, A_W), F32)
    return pl.pallas_call(
        _prep_kernel, grid=grid, in_specs=in_specs,
        out_specs=[tok, tok, tok, tok, both, both, both],
        out_shape=[tok_shape] * 4 + [both_shape] * 3,
        compiler_params=pltpu.CompilerParams(
            dimension_semantics=("arbitrary", "arbitrary"), vmem_limit_bytes=VMEM_LIMIT),
        name="prep",
    )(shift, shift, shift, mu, w0, w2p, a0, a2p, k_k, k_a, r_k, gsum)


def _wkv_kernel(r_ref, v_ref, kk_ref, logw_ref, keff_ref, beta_ref, y_ref, h_ref):
    c = WKV_CHUNK
    rev = pl.program_id(0) == 1
    nchunks = r_ref.shape[1] // c

    @pl.when(pl.program_id(2) == 0)
    def _():
        h_ref[...] = jnp.zeros_like(h_ref)

    row = lax.broadcasted_iota(jnp.int32, (c, c), 0)
    col = lax.broadcasted_iota(jnp.int32, (c, c), 1)
    ahead = (col - row) * (1 - 2 * pl.program_id(0))
    incl = ahead <= 0
    strict = ahead < 0
    tri = incl.astype(F32)
    eye = (row == col).astype(F32)

    def chunk(ci, carry):
        start = pl.multiple_of(jnp.where(rev, nchunks - 1 - ci, ci) * c, c)
        rows = pl.ds(start, c)
        lw = logw_ref[0, 0, rows, :]
        cum = _mm(tri, lw, HIGHEST)
        g_tot = jnp.exp(jnp.sum(lw, axis=0, keepdims=True))
        g_inv = jnp.exp(-cum)
        v = v_ref[0, rows, :]
        qt = r_ref[0, rows, :] * jnp.exp(cum)
        kt = kk_ref[0, rows, :] * jnp.exp(cum - lw)
        kb = keff_ref[0, 0, rows, :] * g_inv
        bb = beta_ref[0, 0, rows, :] * g_inv
        kh = kb * g_tot
        bh = bb * g_tot
        ys = []
        for h in range(A_HEADS):
            sl = slice(h * HEAD_DIM, (h + 1) * HEAD_DIM)
            a_kk = jnp.where(strict, _mm_nt(kt[:, sl], kb[:, sl], HIGHEST), 0.0)
            a_kb = jnp.where(strict, _mm_nt(kt[:, sl], bb[:, sl], HIGHEST), 0.0)
            a_rk = jnp.where(incl, _mm_nt(qt[:, sl], kb[:, sl], HIGHEST), 0.0)
            a_rb = jnp.where(incl, _mm_nt(qt[:, sl], bb[:, sl], HIGHEST), 0.0)
            p = -a_kb
            tinv = eye + p
            for _ in range(int(math.log2(c)) - 1):
                p = _mm(p, p, HIGHEST)
                tinv = tinv + _mm(tinv, p, HIGHEST)
            h0 = h_ref[h]
            vh = v[:, sl]
            u = _mm(tinv, _mm(kt[:, sl], h0, HIGHEST) + _mm(a_kk, vh, HIGHEST), HIGHEST)
            ys.append(_mm(qt[:, sl], h0, HIGHEST) + _mm(a_rk, vh, HIGHEST) - _mm(a_rb, u, HIGHEST))
            h_ref[h] = (_mm(eye * g_tot[:, sl], h0, HIGHEST)
                        + _mm_tn(kh[:, sl], vh, HIGHEST) - _mm_tn(bh[:, sl], u, HIGHEST))
        y_ref[0, 0, rows, :] = jnp.concatenate(ys, axis=1)
        return carry

    lax.fori_loop(0, nchunks, chunk, 0)


def _wkv_call(r, v, kk, logw, keff, beta):
    b, t, _ = r.shape
    tb = WKV_ROWS
    nblk = t // tb
    grid = (2, b, nblk)

    def blk(e, j):
        return jnp.where(e == 0, j, nblk - 1 - j)

    tok = pl.BlockSpec((1, tb, A_W), lambda e, bi, j: (bi, blk(e, j), 0))
    both = pl.BlockSpec((1, 1, tb, A_W), lambda e, bi, j: (e, bi, blk(e, j), 0))
    return pl.pallas_call(
        _wkv_kernel, grid=grid, in_specs=[tok, tok, tok, both, both, both], out_specs=both,
        out_shape=jax.ShapeDtypeStruct((2, b, t, A_W), F32),
        scratch_shapes=[pltpu.VMEM((A_HEADS, HEAD_DIM, HEAD_DIM), F32)],
        compiler_params=pltpu.CompilerParams(
            dimension_semantics=("arbitrary", "arbitrary", "arbitrary"), vmem_limit_bytes=VMEM_LIMIT),
        name="wkv",
    )(r, v, kk, logw, keff, beta)


def _band_kernel(*refs, radius, hq, hkv, seq_len, has_sink, want_lse):
    refs = list(refs)
    sink_ref = refs.pop(0) if has_sink else None
    q_ref, kp_ref, km_ref, kn_ref, vp_ref, vm_ref, vn_ref = refs[:7]
    o_ref = refs[7]
    lse_ref = refs[8] if want_lse else None
    kcat, vcat = refs[-2:]
    rad = radius
    tq = q_ref.shape[1]
    grp = hq // hkv
    i = pl.program_id(1)

    kcat[0:rad] = kp_ref[0]
    kcat[rad:rad + tq] = km_ref[0]
    kcat[rad + tq:rad + tq + rad] = kn_ref[0]
    vcat[0:rad] = vp_ref[0]
    vcat[rad:rad + tq] = vm_ref[0]
    vcat[rad + tq:rad + tq + rad] = vn_ref[0]

    a = lax.broadcasted_iota(jnp.int32, (rad, 3 * rad), 0)
    cc = lax.broadcasted_iota(jnp.int32, (rad, 3 * rad), 1)
    band = jnp.abs(cc - rad - a) <= rad
    for j in range(tq // rad):
        kpos = i * tq + (j - 1) * rad + cc
        mask = band & (kpos >= 0) & (kpos < seq_len)
        qrows = slice(j * rad, (j + 1) * rad)
        krows = slice(j * rad, (j + 3) * rad)
        for g in range(hkv):
            kcols = slice(g * HEAD_DIM, (g + 1) * HEAD_DIM)
            kmat = kcat[krows, kcols]
            vmat = vcat[krows, kcols]
            for hh in range(grp):
                h = g * grp + hh
                qcols = slice(h * HEAD_DIM, (h + 1) * HEAD_DIM)
                s = jnp.where(mask, _mm_nt(q_ref[0, qrows, qcols], kmat), NEG_INF)
                m = jnp.max(s, axis=-1, keepdims=True)
                if has_sink:
                    m = jnp.maximum(m, sink_ref[h])
                p = jnp.exp(s - m)
                denom = jnp.sum(p, axis=-1, keepdims=True)
                if has_sink:
                    denom = denom + jnp.exp(sink_ref[h] - m)
                o_ref[0, qrows, qcols] = _mm(p.astype(BF16), vmat) / denom
                if want_lse:
                    lse_ref[0, qrows, qcols] = jnp.broadcast_to(m + jnp.log(denom), (rad, HEAD_DIM))


def _band_call(q, k, v, *, radius, hq, hkv, sink=None, want_lse=False, name):
    s, l, wq = q.shape
    wk = k.shape[2]
    tq = min(ROW_TILE, l)
    nb = tq // radius
    last = l // radius - 1
    grid = (s, l // tq)
    main_q = pl.BlockSpec((1, tq, wq), lambda si, i: (si, i, 0))
    main_k = pl.BlockSpec((1, tq, wk), lambda si, i: (si, i, 0))
    prev_k = pl.BlockSpec((1, radius, wk), lambda si, i: (si, jnp.maximum(i * nb - 1, 0), 0))
    next_k = pl.BlockSpec((1, radius, wk), lambda si, i: (si, jnp.minimum((i + 1) * nb, last), 0))
    in_specs = [main_q, prev_k, main_k, next_k, prev_k, main_k, next_k]
    args = [q, k, k, k, v, v, v]
    if sink is not None:
        in_specs = [pl.BlockSpec(memory_space=pltpu.SMEM)] + in_specs
        args = [sink] + args
    out_specs = [main_q]
    out_shape = [jax.ShapeDtypeStruct((s, l, wq), F32)]
    if want_lse:
        out_specs.append(main_q)
        out_shape.append(jax.ShapeDtypeStruct((s, l, wq), F32))
    kern = functools.partial(_band_kernel, radius=radius, hq=hq, hkv=hkv, seq_len=l,
                             has_sink=sink is not None, want_lse=want_lse)
    return pl.pallas_call(
        kern, grid=grid, in_specs=in_specs, out_specs=out_specs, out_shape=out_shape,
        scratch_shapes=[pltpu.VMEM((tq + 2 * radius, wk), BF16), pltpu.VMEM((tq + 2 * radius, wk), BF16)],
        compiler_params=pltpu.CompilerParams(
            dimension_semantics=("arbitrary", "arbitrary"), vmem_limit_bytes=VMEM_LIMIT),
        name=name,
    )(*args)


def _post_kernel(*refs, final):
    refs = list(refs)
    (x_ref, yf_ref, yb_ref, bonus_ref, ag_ref, o1_ref, l1_ref, o4_ref, l4_ref, o16_ref, l16_ref,
     bg_ref, co_ref, cg_ref, lnw_ref, lnb_ref, gavg_ref, wo_ref) = refs[:18]
    fg_ref = refs[18] if final else None
    out_ref = refs[-5]
    so4, sl4, so16, sl16 = refs[-4:]
    tm = x_ref.shape[1]

    gavg = gavg_ref[...]
    ya = yf_ref[0, 0] + yb_ref[0, 0]
    mu = _mm(ya, gavg, HIGHEST)
    dev = ya - mu
    var = _mm(dev * dev, gavg, HIGHEST)
    yn = dev * lax.rsqrt(var + HEADNORM_EPS) * lnw_ref[...] + lnb_ref[...]
    mix_a = (yn + bonus_ref[0]) * ag_ref[0]

    nl = B_W // LANES

    def interleave(src_ref, dst, d):
        for r in range(d):
            for j in range(nl):
                dst[j, pl.ds(r, tm // d, stride=d), :] = src_ref[0, r, :, j * LANES:(j + 1) * LANES]
        return jnp.concatenate([dst[j] for j in range(nl)], axis=1)

    o4 = interleave(o4_ref, so4, 4)
    l4 = interleave(l4_ref, sl4, 4)
    o16 = interleave(o16_ref, so16, 16)
    l16 = interleave(l16_ref, sl16, 16)
    l1 = l1_ref[0]
    lmax = jnp.maximum(jnp.maximum(l1, l4), l16)
    w1 = jnp.exp(l1 - lmax)
    w4 = jnp.exp(l4 - lmax)
    w16 = jnp.exp(l16 - lmax)
    mix_b = (w1 * o1_ref[0] + w4 * o4 + w16 * o16) / (w1 + w4 + w16) * bg_ref[0]

    mix_c = co_ref[0] * cg_ref[0]
    xn = (x_ref[0]
          + _mm(mix_a.astype(BF16), wo_ref[0:A_W, :])
          + _mm(mix_b.astype(BF16), wo_ref[A_W:A_W + B_W, :])
          + _mm(mix_c.astype(BF16), wo_ref[A_W + B_W:, :]))
    if final:
        xn = xn * lax.rsqrt(jnp.mean(xn * xn, axis=-1, keepdims=True) + RMS_EPS) * fg_ref[...]
    out_ref[0] = xn


def _post_call(x, y, bonus, ag, o1, l1, o4, l4, o16, l16, bg, co, cg, lnw, lnb, gavg, wo_bf16, final_g):
    b, t, _ = x.shape
    tm = ROW_TILE
    grid = (b, t // tm)
    final = final_g is not None

    def row(w):
        return pl.BlockSpec((1, tm, w), lambda bi, i: (bi, i, 0))

    def strided(d):
        return pl.BlockSpec((1, d, tm // d, B_W), lambda bi, i: (bi, 0, i, 0))

    def const(shape):
        return pl.BlockSpec(shape, lambda bi, i: (0,) * len(shape))

    in_specs = [row(D_MODEL),
                pl.BlockSpec((1, 1, tm, A_W), lambda bi, i: (0, bi, i, 0)),
                pl.BlockSpec((1, 1, tm, A_W), lambda bi, i: (1, bi, i, 0)),
                row(A_W), row(A_W), row(B_W), row(B_W), strided(4), strided(4), strided(16), strided(16),
                row(B_W), row(C_QW), row(C_QW),
                const((1, A_W)), const((1, A_W)), const((A_W, A_W)), const((D_MODEL, D_MODEL))]
    args = [x, y, y, bonus, ag, o1, l1, o4, l4, o16, l16, bg, co, cg, lnw, lnb, gavg, wo_bf16]
    if final:
        in_specs.append(const((1, D_MODEL)))
        args.append(final_g)
    return pl.pallas_call(
        functools.partial(_post_kernel, final=final), grid=grid, in_specs=in_specs,
        out_specs=row(D_MODEL), out_shape=jax.ShapeDtypeStruct((b, t, D_MODEL), F32),
        scratch_shapes=[pltpu.VMEM((B_W // LANES, tm, LANES), F32)] * 4,
        compiler_params=pltpu.CompilerParams(
            dimension_semantics=("arbitrary", "arbitrary"), vmem_limit_bytes=VMEM_LIMIT),
        name="post",
    )(*args)


def _rope_tables(t):
    inv = ROPE_THETA ** (-jnp.arange(0, HEAD_DIM, 2, dtype=F32) / HEAD_DIM)
    ang = jnp.arange(t, dtype=F32)[:, None] * inv[None, :]
    cos = jnp.cos(ang)
    sin = jnp.sin(ang)
    reps = LANES // HEAD_DIM
    cos_t = jnp.tile(jnp.concatenate([cos, cos], axis=1), (1, reps))
    sin_t = jnp.tile(jnp.concatenate([-sin, sin], axis=1), (1, reps))
    return cos_t, sin_t


def _head_block_matrix(value):
    idx = np.arange(A_W) // HEAD_DIM
    return jnp.asarray((idx[:, None] == idx[None, :]).astype(np.float32) * value)


def _lora_weights(w2):
    z = jnp.zeros((LORA, A_W), F32)
    top = jnp.concatenate([jnp.concatenate([w2[0], z], axis=1), jnp.concatenate([z, w2[1]], axis=1)], axis=0)
    pad = jnp.zeros((2 * LORA, 2 * A_W), F32)
    return top, pad


def _layer(x, p, tables, gsum, gavg, final_g):
    b, t, _ = x.shape
    cos_t, sin_t = tables
    (shift, ag, bg, cq, ck, cv, cg,
     bq1, bk1, bv1, bq4, bk4, bv4, bq16, bk16, bv16) = _proj_call(x, p["norm_g"], p["w_in"], cos_t, sin_t)

    r, v, kk, bonus, logw, keff, beta = _prep_call(
        shift, p["mu"], p["w0"], p["w2p"], p["a0"], p["a2p"], p["k_k"], p["k_a"], p["r_k"], gsum)
    y = _wkv_call(r, v, kk, logw, keff, beta)

    branch = []
    for (window, d), (q, k, vv) in zip(DILATED_PAIRS, ((bq1, bk1, bv1), (bq4, bk4, bv4), (bq16, bk16, bv16))):
        l = t // d
        o, lse = _band_call(q.reshape(b * d, l, B_W), k.reshape(b * d, l, B_W), vv.reshape(b * d, l, B_W),
                            radius=window // (2 * d), hq=B_HEADS, hkv=B_HEADS, want_lse=True, name=f"dil{d}")
        branch += [o.reshape(b, d, l, B_W) if d > 1 else o, lse.reshape(b, d, l, B_W) if d > 1 else lse]
    (co,) = _band_call(cq, ck, cv, radius=C_RADIUS, hq=C_HEADS, hkv=C_KV_HEADS, sink=p["sink"], name="win")

    return _post_call(x, y, bonus, ag, *branch, bg, co, cg, p["ln_w"], p["ln_b"], gavg, p["w_out"], final_g)


def _trunk(x, layers, final_g, gsum, gavg):
    tables = _rope_tables(x.shape[1])
    for li, p in enumerate(layers):
        x = _layer(x, p, tables, gsum, gavg, final_g if li == len(layers) - 1 else None)
    return x


def kernel(x_prompt, x_sample, norm_g, w_in, tshift_mu, rwkv_w0, rwkv_w2, rwkv_a0, rwkv_a2, rwkv_k_k, rwkv_k_a,
           rwkv_r_k, ln_x_w, ln_x_b, attn_sink, w_out, final_g):
    depth = norm_g.shape[0]
    layers = []
    for l in range(depth):
        w2_top, pad = _lora_weights(rwkv_w2[l])
        a2_top, _ = _lora_weights(rwkv_a2[l])
        layers.append(dict(
            norm_g=norm_g[l][None, :], w_in=w_in[l].astype(BF16), mu=tshift_mu[l][None, :],
            w0=rwkv_w0[l], w2p=jnp.concatenate([w2_top, pad], axis=0),
            a0=rwkv_a0[l], a2p=jnp.concatenate([pad, a2_top], axis=0),
            k_k=rwkv_k_k[l][None, :], k_a=rwkv_k_a[l][None, :], r_k=rwkv_r_k[l].reshape(1, A_W),
            ln_w=ln_x_w[l][None, :], ln_b=ln_x_b[l][None, :], sink=attn_sink[l],
            w_out=w_out[l].astype(BF16)))
    gsum = _head_block_matrix(1.0)
    gavg = _head_block_matrix(1.0 / HEAD_DIM)
    fg = final_g[None, :]
    return (_trunk(x_prompt, layers, fg, gsum, gavg), _trunk(x_sample, layers, fg, gsum, gavg))
```

```python
import functools
import math

import jax
import jax.numpy as jnp
import numpy as np
from jax import lax
from jax.experimental import pallas as pl
from jax.experimental.pallas import tpu as pltpu

F32 = jnp.float32
BF16 = jnp.bfloat16
HIGHEST = lax.Precision.HIGHEST

D_MODEL = 1024
HEAD_DIM = 64
A_HEADS = 4
A_W = A_HEADS * HEAD_DIM
LORA = 64
B_HEADS = 4
B_W = B_HEADS * HEAD_DIM
DILATED_PAIRS = ((128, 1), (512, 4), (2048, 16))
C_HEADS = 8
C_KV_HEADS = 2
C_QW = C_HEADS * HEAD_DIM
C_KVW = C_KV_HEADS * HEAD_DIM
C_RADIUS = 128
ROPE_THETA = 10000.0
RMS_EPS = 1e-5
HEADNORM_EPS = 64e-5
NEG_INF = -1e30
TSHIFT_W = 3 * A_W + 4 * LORA
IN_W = 3584

COL_AG = TSHIFT_W
COL_BQ = COL_AG + A_W
COL_BK = COL_BQ + B_W
COL_BV = COL_BK + B_W
COL_BG = COL_BV + B_W
COL_CQ = COL_BG + B_W
COL_CK = COL_CQ + C_QW
COL_CV = COL_CK + C_KVW
COL_CG = COL_CV + C_KVW

LANES = 128
ROW_TILE = 512
WKV_CHUNK = 64
WKV_ROWS = 256
HALO_ROWS = 8
VMEM_LIMIT = 56 * 1024 * 1024


def _dot(a, b, dims, precision=None):
    return lax.dot_general(a, b, (dims, ((), ())), precision=precision, preferred_element_type=F32)


def _mm(a, b, precision=None):
    return _dot(a, b, ((1,), (0,)), precision)


def _mm_nt(a, b, precision=None):
    return _dot(a, b, ((1,), (1,)), precision)


def _mm_tn(a, b, precision=None):
    return _dot(a, b, ((0,), (0,)), precision)


def _sigmoid(x):
    return 1.0 / (1.0 + jnp.exp(-x))


def _silu(x):
    return x * _sigmoid(x)


def _softplus(x):
    return jnp.maximum(x, 0.0) + jnp.log(1.0 + jnp.exp(-jnp.abs(x)))


def _proj_kernel(x_ref, g_ref, w_ref, cos_ref, sin_ref,
                 shift_ref, ag_ref, bg_ref, cq_ref, ck_ref, cv_ref, cg_ref,
                 bq1_ref, bk1_ref, bv1_ref, bq4_ref, bk4_ref, bv4_ref, bq16_ref, bk16_ref, bv16_ref,
                 tmp_ref):
    tm = x_ref.shape[1]
    x = x_ref[0]
    h = x * lax.rsqrt(jnp.mean(x * x, axis=-1, keepdims=True) + RMS_EPS) * g_ref[...]
    hb = h.astype(BF16)

    def proj(c0, c1):
        return _mm(hb, w_ref[:, c0:c1])

    cos = cos_ref[...]
    sin = sin_ref[...]
    lane = lax.broadcasted_iota(jnp.int32, (tm, LANES), 1)
    first_half = (lane % HEAD_DIM) < (HEAD_DIM // 2)

    def rope(t):
        outs = []
        for j in range(t.shape[1] // LANES):
            tj = t[:, j * LANES:(j + 1) * LANES]
            partner = jnp.where(first_half,
                                pltpu.roll(tj, LANES - HEAD_DIM // 2, axis=1),
                                pltpu.roll(tj, HEAD_DIM // 2, axis=1))
            outs.append(tj * cos + partner * sin)
        return jnp.concatenate(outs, axis=1) if len(outs) > 1 else outs[0]

    def deinterleave(val, ref1, ref4, ref16):
        ref1[0] = val.astype(BF16)
        nl = B_W // LANES
        for j in range(nl):
            tmp_ref[j] = val[:, j * LANES:(j + 1) * LANES]
        for d, ref in ((4, ref4), (16, ref16)):
            for r in range(d):
                parts = [tmp_ref[j, pl.ds(r, tm // d, stride=d), :] for j in range(nl)]
                ref[0, r] = jnp.concatenate(parts, axis=1).astype(BF16)

    for c0 in range(0, TSHIFT_W, 256):
        shift_ref[0, :, c0:c0 + 256] = proj(c0, c0 + 256)
    ag_ref[0] = _silu(proj(COL_AG, COL_AG + A_W))
    bg_ref[0] = _silu(proj(COL_BG, COL_BG + B_W))
    cg_ref[0] = _silu(proj(COL_CG, COL_CG + C_QW))
    scale = HEAD_DIM ** -0.5
    deinterleave(rope(proj(COL_BQ, COL_BQ + B_W)) * scale, bq1_ref, bq4_ref, bq16_ref)
    deinterleave(rope(proj(COL_BK, COL_BK + B_W)), bk1_ref, bk4_ref, bk16_ref)
    deinterleave(proj(COL_BV, COL_BV + B_W), bv1_ref, bv4_ref, bv16_ref)
    for c0 in range(0, C_QW, 256):
        cq_ref[0, :, c0:c0 + 256] = (rope(proj(COL_CQ + c0, COL_CQ + c0 + 256)) * scale).astype(BF16)
    ck_ref[0] = rope(proj(COL_CK, COL_CK + C_KVW)).astype(BF16)
    cv_ref[0] = proj(COL_CV, COL_CV + C_KVW).astype(BF16)


def _proj_call(x, g, w_bf16, cos_t, sin_t):
    b, t, _ = x.shape
    tm = ROW_TILE
    grid = (b, t // tm)

    def row(w):
        return pl.BlockSpec((1, tm, w), lambda bi, i: (bi, i, 0))

    def strided(d):
        return pl.BlockSpec((1, d, tm // d, B_W), lambda bi, i: (bi, 0, i, 0))

    def nat(w, dt):
        return jax.ShapeDtypeStruct((b, t, w), dt)

    def sshape(d):
        return jax.ShapeDtypeStruct((b, d, t // d, B_W), BF16)

    out_shape = [nat(TSHIFT_W, F32), nat(A_W, F32), nat(B_W, F32),
                 nat(C_QW, BF16), nat(C_KVW, BF16), nat(C_KVW, BF16), nat(C_QW, F32),
                 nat(B_W, BF16), nat(B_W, BF16), nat(B_W, BF16),
                 sshape(4), sshape(4), sshape(4), sshape(16), sshape(16), sshape(16)]
    out_specs = [row(TSHIFT_W), row(A_W), row(B_W),
                 row(C_QW), row(C_KVW), row(C_KVW), row(C_QW),
                 row(B_W), row(B_W), row(B_W),
                 strided(4), strided(4), strided(4), strided(16), strided(16), strided(16)]
    in_specs = [row(D_MODEL),
                pl.BlockSpec((1, D_MODEL), lambda bi, i: (0, 0)),
                pl.BlockSpec((D_MODEL, IN_W), lambda bi, i: (0, 0)),
                pl.BlockSpec((tm, LANES), lambda bi, i: (i, 0)),
                pl.BlockSpec((tm, LANES), lambda bi, i: (i, 0))]
    return pl.pallas_call(
        _proj_kernel, grid=grid, in_specs=in_specs, out_specs=out_specs, out_shape=out_shape,
        scratch_shapes=[pltpu.VMEM((B_W // LANES, tm, LANES), F32)],
        compiler_params=pltpu.CompilerParams(
            dimension_semantics=("arbitrary", "arbitrary"), vmem_limit_bytes=VMEM_LIMIT),
        name="proj",
    )(x, g, w_bf16, cos_t, sin_t)


def _prep_kernel(f_ref, fp_ref, fn_ref, mu_ref, w0_ref, w2_ref, a0_ref, a2_ref, kk_ref_, ka_ref, rk_ref, gsum_ref,
                 r_ref, v_ref, kkn_ref, bonus_ref, logw_ref, keff_ref, beta_ref):
    i = pl.program_id(1)
    n = pl.num_programs(1)
    tp = f_ref.shape[1]
    f = f_ref[0]
    row = lax.broadcasted_iota(jnp.int32, (tp, 1), 0)
    prev_row = jnp.where(i > 0, fp_ref[0, HALO_ROWS - 1:HALO_ROWS, :], 0.0)
    next_row = jnp.where(i < n - 1, fn_ref[0, 0:1, :], 0.0)
    prev = jnp.where(row == 0, prev_row, pltpu.roll(f, 1, axis=0))
    nxt = jnp.where(row == tp - 1, next_row, pltpu.roll(f, tp - 1, axis=0))
    s = f + mu_ref[...] * (0.5 * (prev + nxt) - f)

    r = s[:, 0:A_W]
    k = s[:, A_W:2 * A_W]
    v = s[:, 2 * A_W:3 * A_W]
    lora_in = s[:, 3 * A_W:TSHIFT_W]
    wl = _mm(jnp.tanh(lora_in), w2_ref[...], HIGHEST)
    al = _mm(lora_in, a2_ref[...], HIGHEST)
    gsum = gsum_ref[...]

    kk0 = k * kk_ref_[...]
    ss = _mm(kk0 * kk0, gsum, HIGHEST)
    kkn = kk0 / jnp.maximum(jnp.sqrt(ss), 1e-12)
    r_ref[0] = r
    v_ref[0] = v
    kkn_ref[0] = kkn
    keff_sum = jnp.zeros_like(k)
    for e in range(2):
        w = -_softplus(-(w0_ref[e:e + 1, :] + wl[:, e * A_W:(e + 1) * A_W])) - 0.5
        logw_ref[e, 0] = -jnp.exp(w)
        a = _sigmoid(a0_ref[e:e + 1, :] + al[:, e * A_W:(e + 1) * A_W])
        keff = k * (1.0 + (a - 1.0) * ka_ref[...])
        keff_ref[e, 0] = keff
        beta_ref[e, 0] = a * kkn
        keff_sum = keff_sum + keff
    bonus_ref[0] = _mm(r * keff_sum * rk_ref[...], gsum, HIGHEST) * v


def _prep_call(shift, mu, w0, w2p, a0, a2p, k_k, k_a, r_k, gsum):
    b, t, _ = shift.shape
    tp = ROW_TILE
    nb = tp // HALO_ROWS
    last = t // HALO_ROWS - 1
    grid = (b, t // tp)

    def const(shape):
        return pl.BlockSpec(shape, lambda bi, i: (0,) * len(shape))

    in_specs = [pl.BlockSpec((1, tp, TSHIFT_W), lambda bi, i: (bi, i, 0)),
                pl.BlockSpec((1, HALO_ROWS, TSHIFT_W), lambda bi, i: (bi, jnp.maximum(i * nb - 1, 0), 0)),
                pl.BlockSpec((1, HALO_ROWS, TSHIFT_W), lambda bi, i: (bi, jnp.minimum((i + 1) * nb, last), 0)),
                const((1, TSHIFT_W)), const((2, A_W)), const((4 * LORA, 2 * A_W)),
                const((2, A_W)), const((4 * LORA, 2 * A_W)),
                const((1, A_W)), const((1, A_W)), const((1, A_W)), const((A_W, A_W))]
    tok = pl.BlockSpec((1, tp, A_W), lambda bi, i: (bi, i, 0))
    both = pl.BlockSpec((2, 1, tp, A_W), lambda bi, i: (0, bi, i, 0))
    tok_shape = jax.ShapeDtypeStruct((b, t, A_W), F32)
    both_shape = jax.ShapeDtypeStruct((2, b, t, A_W), F32)
    return pl.pallas_call(
        _prep_kernel, grid=grid, in_specs=in_specs,
        out_specs=[tok, tok, tok, tok, both, both, both],
        out_shape=[tok_shape] * 4 + [both_shape] * 3,
        compiler_params=pltpu.CompilerParams(
            dimension_semantics=("arbitrary", "arbitrary"), vmem_limit_bytes=VMEM_LIMIT),
        name="prep",
    )(shift, shift, shift, mu, w0, w2p, a0, a2p, k_k, k_a, r_k, gsum)


def _split_bf16(x, passes):
    hi = x.astype(BF16)
    if passes == 1:
        return (hi,)
    return (hi, (x - hi.astype(F32)).astype(BF16))


def _mmx(a, b, dims, passes, expand=None):
    ap = _split_bf16(a, passes)
    bp = _split_bf16(b, passes)
    if expand is not None:
        bp = tuple(expand(part) for part in bp)
    out = _dot(ap[0], bp[0], dims)
    if passes == 3:
        out = (_dot(ap[1], bp[0], dims) + _dot(ap[0], bp[1], dims)) + out
    return out


_NN = ((1,), (0,))
_NT = ((1,), (1,))
_TN = ((0,), (0,))
WKV_PASSES_SCORES = 1
WKV_PASSES_INVERSE = 1
WKV_PASSES_STATE = 1
WKV_PASSES_VALUES = 1
WKV_INV_BASE = 16


def _wkv_kernel(rf_ref, vf_ref, kkf_ref, lwf_ref, kef_ref, bef_ref,
                rb_ref, vb_ref, kkb_ref, lwb_ref, keb_ref, beb_ref,
                yf_ref, yb_ref, ht_ref):
    c = WKV_CHUNK
    nchunks = rf_ref.shape[1] // c
    nh = A_HEADS

    @pl.when(pl.program_id(1) == 0)
    def _():
        ht_ref[...] = jnp.zeros_like(ht_ref)

    row = lax.broadcasted_iota(jnp.int32, (c, c), 0)
    col = lax.broadcasted_iota(jnp.int32, (c, c), 1)
    row_w = lax.broadcasted_iota(jnp.int32, (c, A_W), 0)
    col_w = lax.broadcasted_iota(jnp.int32, (c, A_W), 1) % c
    blk_r = lax.broadcasted_iota(jnp.int32, (A_W, A_W), 0) // HEAD_DIM
    blk_c = lax.broadcasted_iota(jnp.int32, (A_W, A_W), 1) // HEAD_DIM
    same_head = blk_r == blk_c
    eye_w = (row_w == col_w).astype(F32)

    def blockdiag(x):
        return jnp.where(same_head, jnp.concatenate([x] * nh, axis=0), jnp.zeros((), x.dtype))

    def unit_triangular_inverse(a_kb):
        base = WKV_INV_BASE
        same_base = (row_w // base) == (col_w // base)
        p = jnp.where(same_base, -a_kb, 0.0)
        tinv = eye_w + p
        p = _mmx(p, p, _NN, WKV_PASSES_INVERSE, blockdiag)
        yield
        span = 2
        while span < base:
            last = 2 * span >= base
            both = _mmx(tinv if last else jnp.concatenate([tinv, p], axis=0), p, _NN, WKV_PASSES_INVERSE, blockdiag)
            yield
            tinv = tinv + both[:c]
            if not last:
                p = both[c:]
            span *= 2
        size = base
        while size < c:
            coupling = ((row_w // (2 * size)) == (col_w // (2 * size))) & ((row_w // size) != (col_w // size))
            w = _mmx(jnp.where(coupling, a_kb, 0.0), tinv, _NN, WKV_PASSES_INVERSE, blockdiag)
            yield
            tinv = tinv - _mmx(tinv, w, _NN, WKV_PASSES_INVERSE, blockdiag)
            yield
            size *= 2
        return tinv

    def one_chunk(rev, start, r_ref, v_ref, kk_ref, lw_ref, ke_ref, be_ref, y_ref, ht_box, slot):
        rows = pl.ds(start, c)
        incl = (col >= row) if rev else (col <= row)
        incl_w = (col_w >= row_w) if rev else (col_w <= row_w)
        strict_w = (col_w > row_w) if rev else (col_w < row_w)

        lw = lw_ref[0, 0, rows, :]
        cum = _mm(incl.astype(F32), lw, HIGHEST)
        g_tot = jnp.exp(jnp.sum(lw, axis=0, keepdims=True))
        g_inv = jnp.exp(-cum)
        v = v_ref[0, rows, :]
        qt = r_ref[0, rows, :] * jnp.exp(cum)
        kt = kk_ref[0, rows, :] * jnp.exp(cum - lw)
        kb = ke_ref[0, 0, rows, :] * g_inv
        bb = be_ref[0, 0, rows, :] * g_inv
        kq = jnp.concatenate([kt, qt], axis=0)

        a_k = _mmx(kq, kb, _NT, WKV_PASSES_SCORES, blockdiag)
        yield
        a_b = _mmx(kq, bb, _NT, WKV_PASSES_SCORES, blockdiag)
        yield
        a_kk = jnp.where(strict_w, a_k[:c], 0.0)
        a_rk = jnp.where(incl_w, a_k[c:], 0.0)
        a_kb = jnp.where(strict_w, a_b[:c], 0.0)
        a_rb = jnp.where(incl_w, a_b[c:], 0.0)
        av = _mmx(jnp.concatenate([a_kk, a_rk], axis=0), v, _NN, WKV_PASSES_VALUES, blockdiag)
        yield
        tinv = yield from unit_triangular_inverse(a_kb)

        while ht_box[slot] is None:
            yield
        ht = ht_box[slot]
        hq = _mmx(kq, ht, _NT, WKV_PASSES_STATE)
        yield
        u = _mmx(tinv, hq[:c] + av[:c], _NN, WKV_PASSES_VALUES, blockdiag)
        yield
        upd = _mmx(jnp.concatenate([v, u], axis=0),
                   jnp.concatenate([kb * g_tot, -(bb * g_tot)], axis=0), _TN, WKV_PASSES_STATE)
        ht_box[slot + 1] = ht * g_tot + jnp.where(same_head, upd, 0.0)
        yield
        y_ref[0, rows, :] = hq[c:] + av[c:] - _mmx(a_rb, u, _NN, WKV_PASSES_VALUES, blockdiag)

    boxes = [[ht_ref[0]] + [None] * nchunks, [ht_ref[1]] + [None] * nchunks]
    tasks = []
    for ci in range(nchunks):
        tasks.append(one_chunk(False, ci * c, rf_ref, vf_ref, kkf_ref, lwf_ref, kef_ref, bef_ref, yf_ref,
                               boxes[0], ci))
        tasks.append(one_chunk(True, (nchunks - 1 - ci) * c, rb_ref, vb_ref, kkb_ref, lwb_ref, keb_ref, beb_ref,
                               yb_ref, boxes[1], ci))
    while tasks:
        alive = []
        for task in tasks:
            try:
                next(task)
                alive.append(task)
            except StopIteration:
                pass
        tasks = alive
    ht_ref[0] = boxes[0][nchunks]
    ht_ref[1] = boxes[1][nchunks]


def _wkv_call(r, v, kk, logw, keff, beta):
    b, t, _ = r.shape
    tb = WKV_ROWS
    nblk = t // tb
    grid = (b, nblk)
    tok_f = pl.BlockSpec((1, tb, A_W), lambda bi, j: (bi, j, 0))
    tok_b = pl.BlockSpec((1, tb, A_W), lambda bi, j: (bi, nblk - 1 - j, 0))
    dir_f = pl.BlockSpec((1, 1, tb, A_W), lambda bi, j: (0, bi, j, 0))
    dir_b = pl.BlockSpec((1, 1, tb, A_W), lambda bi, j: (1, bi, nblk - 1 - j, 0))
    out = jax.ShapeDtypeStruct((b, t, A_W), F32)
    return pl.pallas_call(
        _wkv_kernel, grid=grid,
        in_specs=[tok_f, tok_f, tok_f, dir_f, dir_f, dir_f, tok_b, tok_b, tok_b, dir_b, dir_b, dir_b],
        out_specs=[tok_f, tok_b], out_shape=[out, out],
        scratch_shapes=[pltpu.VMEM((2, A_W, A_W), F32)],
        compiler_params=pltpu.CompilerParams(
            dimension_semantics=("arbitrary", "arbitrary"), vmem_limit_bytes=VMEM_LIMIT),
        name="wkv",
    )(r, v, kk, logw, keff, beta, r, v, kk, logw, keff, beta)


def _band_kernel(*refs, radius, hq, hkv, seq_len, has_sink, want_lse):
    refs = list(refs)
    sink_ref = refs.pop(0) if has_sink else None
    q_ref, kp_ref, km_ref, kn_ref, vp_ref, vm_ref, vn_ref = refs[:7]
    o_ref = refs[7]
    lse_ref = refs[8] if want_lse else None
    kcat, vcat = refs[-2:]
    rad = radius
    tq = q_ref.shape[1]
    grp = hq // hkv
    i = pl.program_id(1)

    kcat[0:rad] = kp_ref[0]
    kcat[rad:rad + tq] = km_ref[0]
    kcat[rad + tq:rad + tq + rad] = kn_ref[0]
    vcat[0:rad] = vp_ref[0]
    vcat[rad:rad + tq] = vm_ref[0]
    vcat[rad + tq:rad + tq + rad] = vn_ref[0]

    a = lax.broadcasted_iota(jnp.int32, (rad, 3 * rad), 0)
    cc = lax.broadcasted_iota(jnp.int32, (rad, 3 * rad), 1)
    band = jnp.abs(cc - rad - a) <= rad
    for j in range(tq // rad):
        kpos = i * tq + (j - 1) * rad + cc
        mask = band & (kpos >= 0) & (kpos < seq_len)
        qrows = slice(j * rad, (j + 1) * rad)
        krows = slice(j * rad, (j + 3) * rad)
        for g in range(hkv):
            kcols = slice(g * HEAD_DIM, (g + 1) * HEAD_DIM)
            kmat = kcat[krows, kcols]
            vmat = vcat[krows, kcols]
            for hh in range(grp):
                h = g * grp + hh
                qcols = slice(h * HEAD_DIM, (h + 1) * HEAD_DIM)
                s = jnp.where(mask, _mm_nt(q_ref[0, qrows, qcols], kmat), NEG_INF)
                m = jnp.max(s, axis=-1, keepdims=True)
                if has_sink:
                    m = jnp.maximum(m, sink_ref[h])
                p = jnp.exp(s - m)
                denom = jnp.sum(p, axis=-1, keepdims=True)
                if has_sink:
                    denom = denom + jnp.exp(sink_ref[h] - m)
                o_ref[0, qrows, qcols] = _mm(p.astype(BF16), vmat) / denom
                if want_lse:
                    lse_ref[0, qrows, qcols] = jnp.broadcast_to(m + jnp.log(denom), (rad, HEAD_DIM))


def _band_call(q, k, v, *, radius, hq, hkv, sink=None, want_lse=False, name):
    s, l, wq = q.shape
    wk = k.shape[2]
    tq = min(ROW_TILE, l)
    nb = tq // radius
    last = l // radius - 1
    grid = (s, l // tq)
    main_q = pl.BlockSpec((1, tq, wq), lambda si, i: (si, i, 0))
    main_k = pl.BlockSpec((1, tq, wk), lambda si, i: (si, i, 0))
    prev_k = pl.BlockSpec((1, radius, wk), lambda si, i: (si, jnp.maximum(i * nb - 1, 0), 0))
    next_k = pl.BlockSpec((1, radius, wk), lambda si, i: (si, jnp.minimum((i + 1) * nb, last), 0))
    in_specs = [main_q, prev_k, main_k, next_k, prev_k, main_k, next_k]
    args = [q, k, k, k, v, v, v]
    if sink is not None:
        in_specs = [pl.BlockSpec(memory_space=pltpu.SMEM)] + in_specs
        args = [sink] + args
    out_specs = [main_q]
    out_shape = [jax.ShapeDtypeStruct((s, l, wq), F32)]
    if want_lse:
        out_specs.append(main_q)
        out_shape.append(jax.ShapeDtypeStruct((s, l, wq), F32))
    kern = functools.partial(_band_kernel, radius=radius, hq=hq, hkv=hkv, seq_len=l,
                             has_sink=sink is not None, want_lse=want_lse)
    return pl.pallas_call(
        kern, grid=grid, in_specs=in_specs, out_specs=out_specs, out_shape=out_shape,
        scratch_shapes=[pltpu.VMEM((tq + 2 * radius, wk), BF16), pltpu.VMEM((tq + 2 * radius, wk), BF16)],
        compiler_params=pltpu.CompilerParams(
            dimension_semantics=("arbitrary", "arbitrary"), vmem_limit_bytes=VMEM_LIMIT),
        name=name,
    )(*args)


def _post_kernel(*refs, final):
    refs = list(refs)
    (x_ref, yf_ref, yb_ref, bonus_ref, ag_ref, o1_ref, l1_ref, o4_ref, l4_ref, o16_ref, l16_ref,
     bg_ref, co_ref, cg_ref, lnw_ref, lnb_ref, gavg_ref, wo_ref) = refs[:18]
    fg_ref = refs[18] if final else None
    out_ref = refs[-5]
    so4, sl4, so16, sl16 = refs[-4:]
    tm = x_ref.shape[1]

    gavg = gavg_ref[...]
    ya = yf_ref[0] + yb_ref[0]
    mu = _mm(ya, gavg, HIGHEST)
    dev = ya - mu
    var = _mm(dev * dev, gavg, HIGHEST)
    yn = dev * lax.rsqrt(var + HEADNORM_EPS) * lnw_ref[...] + lnb_ref[...]
    mix_a = (yn + bonus_ref[0]) * ag_ref[0]

    nl = B_W // LANES

    def interleave(src_ref, dst, d):
        for r in range(d):
            for j in range(nl):
                dst[j, pl.ds(r, tm // d, stride=d), :] = src_ref[0, r, :, j * LANES:(j + 1) * LANES]
        return jnp.concatenate([dst[j] for j in range(nl)], axis=1)

    o4 = interleave(o4_ref, so4, 4)
    l4 = interleave(l4_ref, sl4, 4)
    o16 = interleave(o16_ref, so16, 16)
    l16 = interleave(l16_ref, sl16, 16)
    l1 = l1_ref[0]
    lmax = jnp.maximum(jnp.maximum(l1, l4), l16)
    w1 = jnp.exp(l1 - lmax)
    w4 = jnp.exp(l4 - lmax)
    w16 = jnp.exp(l16 - lmax)
    mix_b = (w1 * o1_ref[0] + w4 * o4 + w16 * o16) / (w1 + w4 + w16) * bg_ref[0]

    mix_c = co_ref[0] * cg_ref[0]
    xn = (x_ref[0]
          + _mm(mix_a.astype(BF16), wo_ref[0:A_W, :])
          + _mm(mix_b.astype(BF16), wo_ref[A_W:A_W + B_W, :])
          + _mm(mix_c.astype(BF16), wo_ref[A_W + B_W:, :]))
    if final:
        xn = xn * lax.rsqrt(jnp.mean(xn * xn, axis=-1, keepdims=True) + RMS_EPS) * fg_ref[...]
    out_ref[0] = xn


def _post_call(x, yf, yb, bonus, ag, o1, l1, o4, l4, o16, l16, bg, co, cg, lnw, lnb, gavg, wo_bf16, final_g):
    b, t, _ = x.shape
    tm = ROW_TILE
    grid = (b, t // tm)
    final = final_g is not None

    def row(w):
        return pl.BlockSpec((1, tm, w), lambda bi, i: (bi, i, 0))

    def strided(d):
        return pl.BlockSpec((1, d, tm // d, B_W), lambda bi, i: (bi, 0, i, 0))

    def const(shape):
        return pl.BlockSpec(shape, lambda bi, i: (0,) * len(shape))

    in_specs = [row(D_MODEL), row(A_W), row(A_W),
                row(A_W), row(A_W), row(B_W), row(B_W), strided(4), strided(4), strided(16), strided(16),
                row(B_W), row(C_QW), row(C_QW),
                const((1, A_W)), const((1, A_W)), const((A_W, A_W)), const((D_MODEL, D_MODEL))]
    args = [x, yf, yb, bonus, ag, o1, l1, o4, l4, o16, l16, bg, co, cg, lnw, lnb, gavg, wo_bf16]
    if final:
        in_specs.append(const((1, D_MODEL)))
        args.append(final_g)
    return pl.pallas_call(
        functools.partial(_post_kernel, final=final), grid=grid, in_specs=in_specs,
        out_specs=row(D_MODEL), out_shape=jax.ShapeDtypeStruct((b, t, D_MODEL), F32),
        scratch_shapes=[pltpu.VMEM((B_W // LANES, tm, LANES), F32)] * 4,
        compiler_params=pltpu.CompilerParams(
            dimension_semantics=("arbitrary", "arbitrary"), vmem_limit_bytes=VMEM_LIMIT),
        name="post",
    )(*args)


def _rope_tables(t):
    inv = ROPE_THETA ** (-jnp.arange(0, HEAD_DIM, 2, dtype=F32) / HEAD_DIM)
    ang = jnp.arange(t, dtype=F32)[:, None] * inv[None, :]
    cos = jnp.cos(ang)
    sin = jnp.sin(ang)
    reps = LANES // HEAD_DIM
    cos_t = jnp.tile(jnp.concatenate([cos, cos], axis=1), (1, reps))
    sin_t = jnp.tile(jnp.concatenate([-sin, sin], axis=1), (1, reps))
    return cos_t, sin_t


def _head_block_matrix(value):
    idx = np.arange(A_W) // HEAD_DIM
    return jnp.asarray((idx[:, None] == idx[None, :]).astype(np.float32) * value)


def _lora_weights(w2):
    z = jnp.zeros((LORA, A_W), F32)
    top = jnp.concatenate([jnp.concatenate([w2[0], z], axis=1), jnp.concatenate([z, w2[1]], axis=1)], axis=0)
    pad = jnp.zeros((2 * LORA, 2 * A_W), F32)
    return top, pad


def _layer(x, p, tables, gsum, gavg, final_g):
    b, t, _ = x.shape
    cos_t, sin_t = tables
    (shift, ag, bg, cq, ck, cv, cg,
     bq1, bk1, bv1, bq4, bk4, bv4, bq16, bk16, bv16) = _proj_call(x, p["norm_g"], p["w_in"], cos_t, sin_t)

    r, v, kk, bonus, logw, keff, beta = _prep_call(
        shift, p["mu"], p["w0"], p["w2p"], p["a0"], p["a2p"], p["k_k"], p["k_a"], p["r_k"], gsum)
    yf, yb = _wkv_call(r, v, kk, logw, keff, beta)

    branch = []
    for (window, d), (q, k, vv) in zip(DILATED_PAIRS, ((bq1, bk1, bv1), (bq4, bk4, bv4), (bq16, bk16, bv16))):
        l = t // d
        o, lse = _band_call(q.reshape(b * d, l, B_W), k.reshape(b * d, l, B_W), vv.reshape(b * d, l, B_W),
                            radius=window // (2 * d), hq=B_HEADS, hkv=B_HEADS, want_lse=True, name=f"dil{d}")
        branch += [o.reshape(b, d, l, B_W) if d > 1 else o, lse.reshape(b, d, l, B_W) if d > 1 else lse]
    (co,) = _band_call(cq, ck, cv, radius=C_RADIUS, hq=C_HEADS, hkv=C_KV_HEADS, sink=p["sink"], name="win")

    return _post_call(x, yf, yb, bonus, ag, *branch, bg, co, cg, p["ln_w"], p["ln_b"], gavg, p["w_out"], final_g)


def _trunk(x, layers, final_g, gsum, gavg):
    tables = _rope_tables(x.shape[1])
    for li, p in enumerate(layers):
        x = _layer(x, p, tables, gsum, gavg, final_g if li == len(layers) - 1 else None)
    return x


def kernel(x_prompt, x_sample, norm_g, w_in, tshift_mu, rwkv_w0, rwkv_w2, rwkv_a0, rwkv_a2, rwkv_k_k, rwkv_k_a,
           rwkv_r_k, ln_x_w, ln_x_b, attn_sink, w_out, final_g):
    depth = norm_g.shape[0]
    layers = []
    for l in range(depth):
        w2_top, pad = _lora_weights(rwkv_w2[l])
        a2_top, _ = _lora_weights(rwkv_a2[l])
        layers.append(dict(
            norm_g=norm_g[l][None, :], w_in=w_in[l].astype(BF16), mu=tshift_mu[l][None, :],
            w0=rwkv_w0[l], w2p=jnp.concatenate([w2_top, pad], axis=0),
            a0=rwkv_a0[l], a2p=jnp.concatenate([pad, a2_top], axis=0),
            k_k=rwkv_k_k[l][None, :], k_a=rwkv_k_a[l][None, :], r_k=rwkv_r_k[l].reshape(1, A_W),
            ln_w=ln_x_w[l][None, :], ln_b=ln_x_b[l][None, :], sink=attn_sink[l],
            w_out=w_out[l].astype(BF16)))
    gsum = _head_block_matrix(1.0)
    gavg = _head_block_matrix(1.0 / HEAD_DIM)
    fg = final_g[None, :]
    return (_trunk(x_prompt, layers, fg, gsum, gavg), _trunk(x_sample, layers, fg, gsum, gavg))
```

```python
import functools
import math

import jax
import jax.numpy as jnp
import numpy as np
from jax import lax
from jax.experimental import pallas as pl
from jax.experimental.pallas import tpu as pltpu

F32 = jnp.float32
BF16 = jnp.bfloat16
HIGHEST = lax.Precision.HIGHEST

D_MODEL = 1024
HEAD_DIM = 64
A_HEADS = 4
A_W = A_HEADS * HEAD_DIM
LORA = 64
B_HEADS = 4
B_W = B_HEADS * HEAD_DIM
DILATED_PAIRS = ((128, 1), (512, 4), (2048, 16))
C_HEADS = 8
C_KV_HEADS = 2
C_QW = C_HEADS * HEAD_DIM
C_KVW = C_KV_HEADS * HEAD_DIM
C_RADIUS = 128
ROPE_THETA = 10000.0
RMS_EPS = 1e-5
HEADNORM_EPS = 64e-5
NEG_INF = -1e30
TSHIFT_W = 3 * A_W + 4 * LORA
IN_W = 3584

COL_AG = TSHIFT_W
COL_BQ = COL_AG + A_W
COL_BK = COL_BQ + B_W
COL_BV = COL_BK + B_W
COL_BG = COL_BV + B_W
COL_CQ = COL_BG + B_W
COL_CK = COL_CQ + C_QW
COL_CV = COL_CK + C_KVW
COL_CG = COL_CV + C_KVW

LANES = 128
ROW_TILE = 512
WKV_CHUNK = 64
WKV_ROWS = 256
HALO_ROWS = 8
BAND_GROUP_ROWS = 512
VMEM_LIMIT = 56 * 1024 * 1024


def _dot(a, b, dims, precision=None):
    return lax.dot_general(a, b, (dims, ((), ())), precision=precision, preferred_element_type=F32)


def _mm(a, b, precision=None):
    return _dot(a, b, ((1,), (0,)), precision)


def _mm_nt(a, b, precision=None):
    return _dot(a, b, ((1,), (1,)), precision)


def _mm_tn(a, b, precision=None):
    return _dot(a, b, ((0,), (0,)), precision)


def _sigmoid(x):
    return 1.0 / (1.0 + jnp.exp(-x))


def _silu(x):
    return x * _sigmoid(x)


def _softplus(x):
    return jnp.maximum(x, 0.0) + jnp.log(1.0 + jnp.exp(-jnp.abs(x)))


def _proj_kernel(x_ref, g_ref, w_ref, cos_ref, sin_ref,
                 shift_ref, ag_ref, bg_ref, cq_ref, ck_ref, cv_ref, cg_ref,
                 bq1_ref, bk1_ref, bv1_ref, bq4_ref, bk4_ref, bv4_ref, bq16_ref, bk16_ref, bv16_ref,
                 tmp_ref):
    tm = x_ref.shape[1]
    x = x_ref[0]
    h = x * lax.rsqrt(jnp.mean(x * x, axis=-1, keepdims=True) + RMS_EPS) * g_ref[...]
    hb = h.astype(BF16)

    def proj(c0, c1):
        return _mm(hb, w_ref[:, c0:c1])

    cos = cos_ref[...]
    sin = sin_ref[...]
    lane = lax.broadcasted_iota(jnp.int32, (tm, LANES), 1)
    first_half = (lane % HEAD_DIM) < (HEAD_DIM // 2)

    def rope(t):
        outs = []
        for j in range(t.shape[1] // LANES):
            tj = t[:, j * LANES:(j + 1) * LANES]
            partner = jnp.where(first_half,
                                pltpu.roll(tj, LANES - HEAD_DIM // 2, axis=1),
                                pltpu.roll(tj, HEAD_DIM // 2, axis=1))
            outs.append(tj * cos + partner * sin)
        return jnp.concatenate(outs, axis=1) if len(outs) > 1 else outs[0]

    def deinterleave(val, ref1, ref4, ref16):
        ref1[0] = val.astype(BF16)
        nl = B_W // LANES
        for j in range(nl):
            tmp_ref[j] = val[:, j * LANES:(j + 1) * LANES]
        for d, ref in ((4, ref4), (16, ref16)):
            for r in range(d):
                parts = [tmp_ref[j, pl.ds(r, tm // d, stride=d), :] for j in range(nl)]
                ref[0, r] = jnp.concatenate(parts, axis=1).astype(BF16)

    for c0 in range(0, TSHIFT_W, 256):
        shift_ref[0, :, c0:c0 + 256] = proj(c0, c0 + 256)
    ag_ref[0] = _silu(proj(COL_AG, COL_AG + A_W))
    bg_ref[0] = _silu(proj(COL_BG, COL_BG + B_W))
    cg_ref[0] = _silu(proj(COL_CG, COL_CG + C_QW))
    scale = HEAD_DIM ** -0.5
    deinterleave(rope(proj(COL_BQ, COL_BQ + B_W)) * scale, bq1_ref, bq4_ref, bq16_ref)
    deinterleave(rope(proj(COL_BK, COL_BK + B_W)), bk1_ref, bk4_ref, bk16_ref)
    deinterleave(proj(COL_BV, COL_BV + B_W), bv1_ref, bv4_ref, bv16_ref)
    for c0 in range(0, C_QW, 256):
        cq_ref[0, :, c0:c0 + 256] = (rope(proj(COL_CQ + c0, COL_CQ + c0 + 256)) * scale).astype(BF16)
    ck_ref[0] = rope(proj(COL_CK, COL_CK + C_KVW)).astype(BF16)
    cv_ref[0] = proj(COL_CV, COL_CV + C_KVW).astype(BF16)


def _proj_call(x, g, w_bf16, cos_t, sin_t):
    b, t, _ = x.shape
    tm = ROW_TILE
    grid = (b, t // tm)

    def row(w):
        return pl.BlockSpec((1, tm, w), lambda bi, i: (bi, i, 0))

    def strided(d):
        return pl.BlockSpec((1, d, tm // d, B_W), lambda bi, i: (bi, 0, i, 0))

    def nat(w, dt):
        return jax.ShapeDtypeStruct((b, t, w), dt)

    def sshape(d):
        return jax.ShapeDtypeStruct((b, d, t // d, B_W), BF16)

    out_shape = [nat(TSHIFT_W, F32), nat(A_W, F32), nat(B_W, F32),
                 nat(C_QW, BF16), nat(C_KVW, BF16), nat(C_KVW, BF16), nat(C_QW, F32),
                 nat(B_W, BF16), nat(B_W, BF16), nat(B_W, BF16),
                 sshape(4), sshape(4), sshape(4), sshape(16), sshape(16), sshape(16)]
    out_specs = [row(TSHIFT_W), row(A_W), row(B_W),
                 row(C_QW), row(C_KVW), row(C_KVW), row(C_QW),
                 row(B_W), row(B_W), row(B_W),
                 strided(4), strided(4), strided(4), strided(16), strided(16), strided(16)]
    in_specs = [row(D_MODEL),
                pl.BlockSpec((1, D_MODEL), lambda bi, i: (0, 0)),
                pl.BlockSpec((D_MODEL, IN_W), lambda bi, i: (0, 0)),
                pl.BlockSpec((tm, LANES), lambda bi, i: (i, 0)),
                pl.BlockSpec((tm, LANES), lambda bi, i: (i, 0))]
    return pl.pallas_call(
        _proj_kernel, grid=grid, in_specs=in_specs, out_specs=out_specs, out_shape=out_shape,
        scratch_shapes=[pltpu.VMEM((B_W // LANES, tm, LANES), F32)],
        compiler_params=pltpu.CompilerParams(
            dimension_semantics=("arbitrary", "arbitrary"), vmem_limit_bytes=VMEM_LIMIT),
        name="proj",
    )(x, g, w_bf16, cos_t, sin_t)


def _prep_kernel(f_ref, fp_ref, fn_ref, mu_ref, w0_ref, w2_ref, a0_ref, a2_ref, kk_ref_, ka_ref, rk_ref, gsum_ref,
                 r_ref, v_ref, kkn_ref, bonus_ref, logw_ref, keff_ref, beta_ref):
    i = pl.program_id(1)
    n = pl.num_programs(1)
    tp = f_ref.shape[1]
    f = f_ref[0]
    row = lax.broadcasted_iota(jnp.int32, (tp, 1), 0)
    prev_row = jnp.where(i > 0, fp_ref[0, HALO_ROWS - 1:HALO_ROWS, :], 0.0)
    next_row = jnp.where(i < n - 1, fn_ref[0, 0:1, :], 0.0)
    prev = jnp.where(row == 0, prev_row, pltpu.roll(f, 1, axis=0))
    nxt = jnp.where(row == tp - 1, next_row, pltpu.roll(f, tp - 1, axis=0))
    s = f + mu_ref[...] * (0.5 * (prev + nxt) - f)

    r = s[:, 0:A_W]
    k = s[:, A_W:2 * A_W]
    v = s[:, 2 * A_W:3 * A_W]
    lora_in = s[:, 3 * A_W:TSHIFT_W]
    wl = _mm(jnp.tanh(lora_in), w2_ref[...], HIGHEST)
    al = _mm(lora_in, a2_ref[...], HIGHEST)
    gsum = gsum_ref[...]

    kk0 = k * kk_ref_[...]
    ss = _mm(kk0 * kk0, gsum, HIGHEST)
    kkn = kk0 / jnp.maximum(jnp.sqrt(ss), 1e-12)
    r_ref[0] = r
    v_ref[0] = v
    kkn_ref[0] = kkn
    keff_sum = jnp.zeros_like(k)
    for e in range(2):
        w = -_softplus(-(w0_ref[e:e + 1, :] + wl[:, e * A_W:(e + 1) * A_W])) - 0.5
        logw_ref[e, 0] = -jnp.exp(w)
        a = _sigmoid(a0_ref[e:e + 1, :] + al[:, e * A_W:(e + 1) * A_W])
        keff = k * (1.0 + (a - 1.0) * ka_ref[...])
        keff_ref[e, 0] = keff
        beta_ref[e, 0] = a * kkn
        keff_sum = keff_sum + keff
    bonus_ref[0] = _mm(r * keff_sum * rk_ref[...], gsum, HIGHEST) * v


def _prep_call(shift, mu, w0, w2p, a0, a2p, k_k, k_a, r_k, gsum):
    b, t, _ = shift.shape
    tp = ROW_TILE
    nb = tp // HALO_ROWS
    last = t // HALO_ROWS - 1
    grid = (b, t // tp)

    def const(shape):
        return pl.BlockSpec(shape, lambda bi, i: (0,) * len(shape))

    in_specs = [pl.BlockSpec((1, tp, TSHIFT_W), lambda bi, i: (bi, i, 0)),
                pl.BlockSpec((1, HALO_ROWS, TSHIFT_W), lambda bi, i: (bi, jnp.maximum(i * nb - 1, 0), 0)),
                pl.BlockSpec((1, HALO_ROWS, TSHIFT_W), lambda bi, i: (bi, jnp.minimum((i + 1) * nb, last), 0)),
                const((1, TSHIFT_W)), const((2, A_W)), const((4 * LORA, 2 * A_W)),
                const((2, A_W)), const((4 * LORA, 2 * A_W)),
                const((1, A_W)), const((1, A_W)), const((1, A_W)), const((A_W, A_W))]
    tok = pl.BlockSpec((1, tp, A_W), lambda bi, i: (bi, i, 0))
    both = pl.BlockSpec((2, 1, tp, A_W), lambda bi, i: (0, bi, i, 0))
    tok_shape = jax.ShapeDtypeStruct((b, t, A_W), F32)
    both_shape = jax.ShapeDtypeStruct((2, b, t, A_W), F32)
    return pl.pallas_call(
        _prep_kernel, grid=grid, in_specs=in_specs,
        out_specs=[tok, tok, tok, tok, both, both, both],
        out_shape=[tok_shape] * 4 + [both_shape] * 3,
        compiler_params=pltpu.CompilerParams(
            dimension_semantics=("arbitrary", "arbitrary"), vmem_limit_bytes=VMEM_LIMIT),
        name="prep",
    )(shift, shift, shift, mu, w0, w2p, a0, a2p, k_k, k_a, r_k, gsum)


def _split_bf16(x, passes):
    hi = x.astype(BF16)
    if passes == 1:
        return (hi,)
    return (hi, (x - hi.astype(F32)).astype(BF16))


def _mmx(a, b, dims, passes, expand=None):
    ap = _split_bf16(a, passes)
    bp = _split_bf16(b, passes)
    if expand is not None:
        bp = tuple(expand(part) for part in bp)
    out = _dot(ap[0], bp[0], dims)
    if passes == 3:
        out = (_dot(ap[1], bp[0], dims) + _dot(ap[0], bp[1], dims)) + out
    return out


_NN = ((1,), (0,))
_NT = ((1,), (1,))
_TN = ((0,), (0,))
WKV_PASSES_SCORES = 1
WKV_PASSES_INVERSE = 1
WKV_PASSES_STATE = 1
WKV_PASSES_VALUES = 1
WKV_INV_BASE = 16


def _wkv_kernel(rf_ref, vf_ref, kkf_ref, lwf_ref, kef_ref, bef_ref,
                rb_ref, vb_ref, kkb_ref, lwb_ref, keb_ref, beb_ref,
                yf_ref, yb_ref, ht_ref):
    c = WKV_CHUNK
    nchunks = rf_ref.shape[1] // c
    nh = A_HEADS

    @pl.when(pl.program_id(1) == 0)
    def _():
        ht_ref[...] = jnp.zeros_like(ht_ref)

    row = lax.broadcasted_iota(jnp.int32, (c, c), 0)
    col = lax.broadcasted_iota(jnp.int32, (c, c), 1)
    row_w = lax.broadcasted_iota(jnp.int32, (c, A_W), 0)
    col_w = lax.broadcasted_iota(jnp.int32, (c, A_W), 1) % c
    blk_r = lax.broadcasted_iota(jnp.int32, (A_W, A_W), 0) // HEAD_DIM
    blk_c = lax.broadcasted_iota(jnp.int32, (A_W, A_W), 1) // HEAD_DIM
    same_head = blk_r == blk_c
    eye_w = (row_w == col_w).astype(F32)

    def blockdiag(x):
        return jnp.where(same_head, jnp.concatenate([x] * nh, axis=0), jnp.zeros((), x.dtype))

    def unit_triangular_inverse(a_kb):
        base = WKV_INV_BASE
        same_base = (row_w // base) == (col_w // base)
        p = jnp.where(same_base, -a_kb, 0.0)
        tinv = eye_w + p
        p = _mmx(p, p, _NN, WKV_PASSES_INVERSE, blockdiag)
        yield
        span = 2
        while span < base:
            last = 2 * span >= base
            both = _mmx(tinv if last else jnp.concatenate([tinv, p], axis=0), p, _NN, WKV_PASSES_INVERSE, blockdiag)
            yield
            tinv = tinv + both[:c]
            if not last:
                p = both[c:]
            span *= 2
        size = base
        while size < c:
            coupling = ((row_w // (2 * size)) == (col_w // (2 * size))) & ((row_w // size) != (col_w // size))
            w = _mmx(jnp.where(coupling, a_kb, 0.0), tinv, _NN, WKV_PASSES_INVERSE, blockdiag)
            yield
            tinv = tinv - _mmx(tinv, w, _NN, WKV_PASSES_INVERSE, blockdiag)
            yield
            size *= 2
        return tinv

    def one_chunk(rev, start, r_ref, v_ref, kk_ref, lw_ref, ke_ref, be_ref, y_ref, ht_box, slot):
        rows = pl.ds(start, c)
        incl = (col >= row) if rev else (col <= row)
        incl_w = (col_w >= row_w) if rev else (col_w <= row_w)
        strict_w = (col_w > row_w) if rev else (col_w < row_w)

        lw = lw_ref[0, 0, rows, :]
        cum = _mm(incl.astype(F32), lw, HIGHEST)
        g_tot = jnp.exp(jnp.sum(lw, axis=0, keepdims=True))
        g_inv = jnp.exp(-cum)
        v = v_ref[0, rows, :]
        qt = r_ref[0, rows, :] * jnp.exp(cum)
        kt = kk_ref[0, rows, :] * jnp.exp(cum - lw)
        kb = ke_ref[0, 0, rows, :] * g_inv
        bb = be_ref[0, 0, rows, :] * g_inv
        kq = jnp.concatenate([kt, qt], axis=0)

        a_k = _mmx(kq, kb, _NT, WKV_PASSES_SCORES, blockdiag)
        yield
        a_b = _mmx(kq, bb, _NT, WKV_PASSES_SCORES, blockdiag)
        yield
        a_kk = jnp.where(strict_w, a_k[:c], 0.0)
        a_rk = jnp.where(incl_w, a_k[c:], 0.0)
        a_kb = jnp.where(strict_w, a_b[:c], 0.0)
        a_rb = jnp.where(incl_w, a_b[c:], 0.0)
        av = _mmx(jnp.concatenate([a_kk, a_rk], axis=0), v, _NN, WKV_PASSES_VALUES, blockdiag)
        yield
        tinv = yield from unit_triangular_inverse(a_kb)

        while ht_box[slot] is None:
            yield
        ht = ht_box[slot]
        hq = _mmx(kq, ht, _NT, WKV_PASSES_STATE)
        yield
        u = _mmx(tinv, hq[:c] + av[:c], _NN, WKV_PASSES_VALUES, blockdiag)
        yield
        upd = _mmx(jnp.concatenate([v, u], axis=0),
                   jnp.concatenate([kb * g_tot, -(bb * g_tot)], axis=0), _TN, WKV_PASSES_STATE)
        ht_box[slot + 1] = ht * g_tot + jnp.where(same_head, upd, 0.0)
        yield
        y_ref[0, rows, :] = hq[c:] + av[c:] - _mmx(a_rb, u, _NN, WKV_PASSES_VALUES, blockdiag)

    boxes = [[ht_ref[0]] + [None] * nchunks, [ht_ref[1]] + [None] * nchunks]
    tasks = []
    for ci in range(nchunks):
        tasks.append(one_chunk(False, ci * c, rf_ref, vf_ref, kkf_ref, lwf_ref, kef_ref, bef_ref, yf_ref,
                               boxes[0], ci))
        tasks.append(one_chunk(True, (nchunks - 1 - ci) * c, rb_ref, vb_ref, kkb_ref, lwb_ref, keb_ref, beb_ref,
                               yb_ref, boxes[1], ci))
    while tasks:
        alive = []
        for task in tasks:
            try:
                next(task)
                alive.append(task)
            except StopIteration:
                pass
        tasks = alive
    ht_ref[0] = boxes[0][nchunks]
    ht_ref[1] = boxes[1][nchunks]


def _wkv_call(r, v, kk, logw, keff, beta):
    b, t, _ = r.shape
    tb = WKV_ROWS
    nblk = t // tb
    grid = (b, nblk)
    tok_f = pl.BlockSpec((1, tb, A_W), lambda bi, j: (bi, j, 0))
    tok_b = pl.BlockSpec((1, tb, A_W), lambda bi, j: (bi, nblk - 1 - j, 0))
    dir_f = pl.BlockSpec((1, 1, tb, A_W), lambda bi, j: (0, bi, j, 0))
    dir_b = pl.BlockSpec((1, 1, tb, A_W), lambda bi, j: (1, bi, nblk - 1 - j, 0))
    out = jax.ShapeDtypeStruct((b, t, A_W), F32)
    return pl.pallas_call(
        _wkv_kernel, grid=grid,
        in_specs=[tok_f, tok_f, tok_f, dir_f, dir_f, dir_f, tok_b, tok_b, tok_b, dir_b, dir_b, dir_b],
        out_specs=[tok_f, tok_b], out_shape=[out, out],
        scratch_shapes=[pltpu.VMEM((2, A_W, A_W), F32)],
        compiler_params=pltpu.CompilerParams(
            dimension_semantics=("arbitrary", "arbitrary"), vmem_limit_bytes=VMEM_LIMIT),
        name="wkv",
    )(r, v, kk, logw, keff, beta, r, v, kk, logw, keff, beta)


def _band_kernel(*refs, radius, hq, hkv, seq_len, has_sink, want_lse, group):
    refs = list(refs)
    sink_ref = refs.pop(0) if has_sink else None
    q_ref, kp_ref, km_ref, kn_ref, vp_ref, vm_ref, vn_ref = refs[:7]
    o_ref = refs[7]
    lse_ref = refs[8] if want_lse else None
    kcat, vcat = refs[-2:]
    rad = radius
    tq = q_ref.shape[1]
    grp = hq // hkv
    i = pl.program_id(1)

    kcat[0:rad] = kp_ref[0]
    kcat[rad:rad + tq] = km_ref[0]
    kcat[rad + tq:rad + tq + rad] = kn_ref[0]
    vcat[0:rad] = vp_ref[0]
    vcat[rad:rad + tq] = vm_ref[0]
    vcat[rad + tq:rad + tq + rad] = vn_ref[0]

    nq = tq // rad
    a = lax.broadcasted_iota(jnp.int32, (rad, 3 * rad), 0)
    cc = lax.broadcasted_iota(jnp.int32, (rad, 3 * rad), 1)
    band_bias = jnp.where(jnp.abs(cc - rad - a) <= rad, 0.0, NEG_INF)

    def bias_for(j):
        bias = band_bias
        kpos = i * tq + (j - 1) * rad + cc
        if j == 0:
            bias = bias + jnp.where(kpos >= 0, 0.0, NEG_INF)
        if j == nq - 1:
            bias = bias + jnp.where(kpos < seq_len, 0.0, NEG_INF)
        return bias

    def scores(j, g, hh, bias):
        h = g * grp + hh
        q = q_ref[0, j * rad:(j + 1) * rad, h * HEAD_DIM:(h + 1) * HEAD_DIM]
        kmat = kcat[j * rad:(j + 3) * rad, g * HEAD_DIM:(g + 1) * HEAD_DIM]
        s = _mm_nt(q, kmat) + bias
        m = jnp.max(s, axis=-1, keepdims=True)
        if has_sink:
            m = jnp.maximum(m, sink_ref[h])
        p = jnp.exp(s - m)
        denom = jnp.sum(p, axis=-1, keepdims=True)
        if has_sink:
            denom = denom + jnp.exp(sink_ref[h] - m)
        return p.astype(BF16), m, denom

    def values(j, g, hh, p, m, denom):
        h = g * grp + hh
        qrows = slice(j * rad, (j + 1) * rad)
        qcols = slice(h * HEAD_DIM, (h + 1) * HEAD_DIM)
        vmat = vcat[j * rad:(j + 3) * rad, g * HEAD_DIM:(g + 1) * HEAD_DIM]
        o_ref[0, qrows, qcols] = _mm(p, vmat) / denom
        if want_lse:
            lse_ref[0, qrows, qcols] = jnp.broadcast_to(m + jnp.log(denom), (rad, HEAD_DIM))

    bodies = [(j, g, hh) for j in range(nq) for g in range(hkv) for hh in range(grp)]
    groups = [bodies[n:n + group] for n in range(0, len(bodies), group)]
    biases = {}

    def run_scores(grp_bodies):
        out = []
        for (j, g, hh) in grp_bodies:
            if j not in biases:
                biases[j] = bias_for(j)
            out.append(scores(j, g, hh, biases[j]))
        return out

    pending = run_scores(groups[0])
    for n, grp_bodies in enumerate(groups):
        nxt = run_scores(groups[n + 1]) if n + 1 < len(groups) else None
        for body, state in zip(grp_bodies, pending):
            values(*body, *state)
        pending = nxt


def _band_call(q, k, v, *, radius, hq, hkv, sink=None, want_lse=False, name):
    s, l, wq = q.shape
    wk = k.shape[2]
    tq = min(ROW_TILE, l)
    nb = tq // radius
    last = l // radius - 1
    grid = (s, l // tq)
    main_q = pl.BlockSpec((1, tq, wq), lambda si, i: (si, i, 0))
    main_k = pl.BlockSpec((1, tq, wk), lambda si, i: (si, i, 0))
    prev_k = pl.BlockSpec((1, radius, wk), lambda si, i: (si, jnp.maximum(i * nb - 1, 0), 0))
    next_k = pl.BlockSpec((1, radius, wk), lambda si, i: (si, jnp.minimum((i + 1) * nb, last), 0))
    in_specs = [main_q, prev_k, main_k, next_k, prev_k, main_k, next_k]
    args = [q, k, k, k, v, v, v]
    if sink is not None:
        in_specs = [pl.BlockSpec(memory_space=pltpu.SMEM)] + in_specs
        args = [sink] + args
    out_specs = [main_q]
    out_shape = [jax.ShapeDtypeStruct((s, l, wq), F32)]
    if want_lse:
        out_specs.append(main_q)
        out_shape.append(jax.ShapeDtypeStruct((s, l, wq), F32))
    group = max(1, BAND_GROUP_ROWS // radius)
    kern = functools.partial(_band_kernel, radius=radius, hq=hq, hkv=hkv, seq_len=l,
                             has_sink=sink is not None, want_lse=want_lse, group=group)
    return pl.pallas_call(
        kern, grid=grid, in_specs=in_specs, out_specs=out_specs, out_shape=out_shape,
        scratch_shapes=[pltpu.VMEM((tq + 2 * radius, wk), BF16), pltpu.VMEM((tq + 2 * radius, wk), BF16)],
        compiler_params=pltpu.CompilerParams(
            dimension_semantics=("arbitrary", "arbitrary"), vmem_limit_bytes=VMEM_LIMIT),
        name=name,
    )(*args)


def _post_kernel(*refs, final):
    refs = list(refs)
    (x_ref, yf_ref, yb_ref, bonus_ref, ag_ref, o1_ref, l1_ref, o4_ref, l4_ref, o16_ref, l16_ref,
     bg_ref, co_ref, cg_ref, lnw_ref, lnb_ref, gavg_ref, wo_ref) = refs[:18]
    fg_ref = refs[18] if final else None
    out_ref = refs[-5]
    so4, sl4, so16, sl16 = refs[-4:]
    tm = x_ref.shape[1]

    gavg = gavg_ref[...]
    ya = yf_ref[0] + yb_ref[0]
    mu = _mm(ya, gavg, HIGHEST)
    dev = ya - mu
    var = _mm(dev * dev, gavg, HIGHEST)
    yn = dev * lax.rsqrt(var + HEADNORM_EPS) * lnw_ref[...] + lnb_ref[...]
    mix_a = (yn + bonus_ref[0]) * ag_ref[0]

    nl = B_W // LANES

    def interleave(src_ref, dst, d):
        for r in range(d):
            for j in range(nl):
                dst[j, pl.ds(r, tm // d, stride=d), :] = src_ref[0, r, :, j * LANES:(j + 1) * LANES]
        return jnp.concatenate([dst[j] for j in range(nl)], axis=1)

    o4 = interleave(o4_ref, so4, 4)
    l4 = interleave(l4_ref, sl4, 4)
    o16 = interleave(o16_ref, so16, 16)
    l16 = interleave(l16_ref, sl16, 16)
    l1 = l1_ref[0]
    lmax = jnp.maximum(jnp.maximum(l1, l4), l16)
    w1 = jnp.exp(l1 - lmax)
    w4 = jnp.exp(l4 - lmax)
    w16 = jnp.exp(l16 - lmax)
    mix_b = (w1 * o1_ref[0] + w4 * o4 + w16 * o16) / (w1 + w4 + w16) * bg_ref[0]

    mix_c = co_ref[0] * cg_ref[0]
    xn = (x_ref[0]
          + _mm(mix_a.astype(BF16), wo_ref[0:A_W, :])
          + _mm(mix_b.astype(BF16), wo_ref[A_W:A_W + B_W, :])
          + _mm(mix_c.astype(BF16), wo_ref[A_W + B_W:, :]))
    if final:
        xn = xn * lax.rsqrt(jnp.mean(xn * xn, axis=-1, keepdims=True) + RMS_EPS) * fg_ref[...]
    out_ref[0] = xn


def _post_call(x, yf, yb, bonus, ag, o1, l1, o4, l4, o16, l16, bg, co, cg, lnw, lnb, gavg, wo_bf16, final_g):
    b, t, _ = x.shape
    tm = ROW_TILE
    grid = (b, t // tm)
    final = final_g is not None

    def row(w):
        return pl.BlockSpec((1, tm, w), lambda bi, i: (bi, i, 0))

    def strided(d):
        return pl.BlockSpec((1, d, tm // d, B_W), lambda bi, i: (bi, 0, i, 0))

    def const(shape):
        return pl.BlockSpec(shape, lambda bi, i: (0,) * len(shape))

    in_specs = [row(D_MODEL), row(A_W), row(A_W),
                row(A_W), row(A_W), row(B_W), row(B_W), strided(4), strided(4), strided(16), strided(16),
                row(B_W), row(C_QW), row(C_QW),
                const((1, A_W)), const((1, A_W)), const((A_W, A_W)), const((D_MODEL, D_MODEL))]
    args = [x, yf, yb, bonus, ag, o1, l1, o4, l4, o16, l16, bg, co, cg, lnw, lnb, gavg, wo_bf16]
    if final:
        in_specs.append(const((1, D_MODEL)))
        args.append(final_g)
    return pl.pallas_call(
        functools.partial(_post_kernel, final=final), grid=grid, in_specs=in_specs,
        out_specs=row(D_MODEL), out_shape=jax.ShapeDtypeStruct((b, t, D_MODEL), F32),
        scratch_shapes=[pltpu.VMEM((B_W // LANES, tm, LANES), F32)] * 4,
        compiler_params=pltpu.CompilerParams(
            dimension_semantics=("arbitrary", "arbitrary"), vmem_limit_bytes=VMEM_LIMIT),
        name="post",
    )(*args)


def _rope_tables(t):
    inv = ROPE_THETA ** (-jnp.arange(0, HEAD_DIM, 2, dtype=F32) / HEAD_DIM)
    ang = jnp.arange(t, dtype=F32)[:, None] * inv[None, :]
    cos = jnp.cos(ang)
    sin = jnp.sin(ang)
    reps = LANES // HEAD_DIM
    cos_t = jnp.tile(jnp.concatenate([cos, cos], axis=1), (1, reps))
    sin_t = jnp.tile(jnp.concatenate([-sin, sin], axis=1), (1, reps))
    return cos_t, sin_t


def _head_block_matrix(value):
    idx = np.arange(A_W) // HEAD_DIM
    return jnp.asarray((idx[:, None] == idx[None, :]).astype(np.float32) * value)


def _lora_weights(w2):
    z = jnp.zeros((LORA, A_W), F32)
    top = jnp.concatenate([jnp.concatenate([w2[0], z], axis=1), jnp.concatenate([z, w2[1]], axis=1)], axis=0)
    pad = jnp.zeros((2 * LORA, 2 * A_W), F32)
    return top, pad


def _layer(x, p, tables, gsum, gavg, final_g):
    b, t, _ = x.shape
    cos_t, sin_t = tables
    (shift, ag, bg, cq, ck, cv, cg,
     bq1, bk1, bv1, bq4, bk4, bv4, bq16, bk16, bv16) = _proj_call(x, p["norm_g"], p["w_in"], cos_t, sin_t)

    r, v, kk, bonus, logw, keff, beta = _prep_call(
        shift, p["mu"], p["w0"], p["w2p"], p["a0"], p["a2p"], p["k_k"], p["k_a"], p["r_k"], gsum)
    yf, yb = _wkv_call(r, v, kk, logw, keff, beta)

    branch = []
    for (window, d), (q, k, vv) in zip(DILATED_PAIRS, ((bq1, bk1, bv1), (bq4, bk4, bv4), (bq16, bk16, bv16))):
        l = t // d
        o, lse = _band_call(q.reshape(b * d, l, B_W), k.reshape(b * d, l, B_W), vv.reshape(b * d, l, B_W),
                            radius=window // (2 * d), hq=B_HEADS, hkv=B_HEADS, want_lse=True, name=f"dil{d}")
        branch += [o.reshape(b, d, l, B_W) if d > 1 else o, lse.reshape(b, d, l, B_W) if d > 1 else lse]
    (co,) = _band_call(cq, ck, cv, radius=C_RADIUS, hq=C_HEADS, hkv=C_KV_HEADS, sink=p["sink"], name="win")

    return _post_call(x, yf, yb, bonus, ag, *branch, bg, co, cg, p["ln_w"], p["ln_b"], gavg, p["w_out"], final_g)


def _trunk(x, layers, final_g, gsum, gavg):
    tables = _rope_tables(x.shape[1])
    for li, p in enumerate(layers):
        x = _layer(x, p, tables, gsum, gavg, final_g if li == len(layers) - 1 else None)
    return x


def kernel(x_prompt, x_sample, norm_g, w_in, tshift_mu, rwkv_w0, rwkv_w2, rwkv_a0, rwkv_a2, rwkv_k_k, rwkv_k_a,
           rwkv_r_k, ln_x_w, ln_x_b, attn_sink, w_out, final_g):
    depth = norm_g.shape[0]
    layers = []
    for l in range(depth):
        w2_top, pad = _lora_weights(rwkv_w2[l])
        a2_top, _ = _lora_weights(rwkv_a2[l])
        layers.append(dict(
            norm_g=norm_g[l][None, :], w_in=w_in[l].astype(BF16), mu=tshift_mu[l][None, :],
            w0=rwkv_w0[l], w2p=jnp.concatenate([w2_top, pad], axis=0),
            a0=rwkv_a0[l], a2p=jnp.concatenate([pad, a2_top], axis=0),
            k_k=rwkv_k_k[l][None, :], k_a=rwkv_k_a[l][None, :], r_k=rwkv_r_k[l].reshape(1, A_W),
            ln_w=ln_x_w[l][None, :], ln_b=ln_x_b[l][None, :], sink=attn_sink[l],
            w_out=w_out[l].astype(BF16)))
    gsum = _head_block_matrix(1.0)
    gavg = _head_block_matrix(1.0 / HEAD_DIM)
    fg = final_g[None, :]
    return (_trunk(x_prompt, layers, fg, gsum, gavg), _trunk(x_sample, layers, fg, gsum, gavg))
```

```python
import functools
import math

import jax
import jax.numpy as jnp
import numpy as np
from jax import lax
from jax.experimental import pallas as pl
from jax.experimental.pallas import tpu as pltpu

F32 = jnp.float32
BF16 = jnp.bfloat16
HIGHEST = lax.Precision.HIGHEST

D_MODEL = 1024
HEAD_DIM = 64
A_HEADS = 4
A_W = A_HEADS * HEAD_DIM
LORA = 64
B_HEADS = 4
B_W = B_HEADS * HEAD_DIM
DILATED_PAIRS = ((128, 1), (512, 4), (2048, 16))
C_HEADS = 8
C_KV_HEADS = 2
C_QW = C_HEADS * HEAD_DIM
C_KVW = C_KV_HEADS * HEAD_DIM
C_RADIUS = 128
ROPE_THETA = 10000.0
RMS_EPS = 1e-5
HEADNORM_EPS = 64e-5
NEG_INF = -1e30
TSHIFT_W = 3 * A_W + 4 * LORA
IN_W = 3584

COL_AG = TSHIFT_W
COL_BQ = COL_AG + A_W
COL_BK = COL_BQ + B_W
COL_BV = COL_BK + B_W
COL_BG = COL_BV + B_W
COL_CQ = COL_BG + B_W
COL_CK = COL_CQ + C_QW
COL_CV = COL_CK + C_KVW
COL_CG = COL_CV + C_KVW

LANES = 128
ROW_TILE = 512
WKV_CHUNK = 64
WKV_ROWS = 256
HALO_ROWS = 8
BAND_GROUP_ROWS = 512
BAND_QBLK = 128
VMEM_LIMIT = 56 * 1024 * 1024


def _dot(a, b, dims, precision=None):
    return lax.dot_general(a, b, (dims, ((), ())), precision=precision, preferred_element_type=F32)


def _mm(a, b, precision=None):
    return _dot(a, b, ((1,), (0,)), precision)


def _mm_nt(a, b, precision=None):
    return _dot(a, b, ((1,), (1,)), precision)


def _mm_tn(a, b, precision=None):
    return _dot(a, b, ((0,), (0,)), precision)


def _sigmoid(x):
    return 1.0 / (1.0 + jnp.exp(-x))


def _silu(x):
    return x * _sigmoid(x)


def _softplus(x):
    return jnp.maximum(x, 0.0) + jnp.log(1.0 + jnp.exp(-jnp.abs(x)))


def _proj_kernel(x_ref, g_ref, w_ref, cos_ref, sin_ref,
                 shift_ref, ag_ref, bg_ref, cq_ref, ck_ref, cv_ref, cg_ref,
                 bq1_ref, bk1_ref, bv1_ref, bq4_ref, bk4_ref, bv4_ref, bq16_ref, bk16_ref, bv16_ref,
                 tmp_ref):
    tm = x_ref.shape[1]
    x = x_ref[0]
    h = x * lax.rsqrt(jnp.mean(x * x, axis=-1, keepdims=True) + RMS_EPS) * g_ref[...]
    hb = h.astype(BF16)

    def proj(c0, c1):
        return _mm(hb, w_ref[:, c0:c1])

    cos = cos_ref[...]
    sin = sin_ref[...]
    lane = lax.broadcasted_iota(jnp.int32, (tm, LANES), 1)
    first_half = (lane % HEAD_DIM) < (HEAD_DIM // 2)

    def rope(t):
        outs = []
        for j in range(t.shape[1] // LANES):
            tj = t[:, j * LANES:(j + 1) * LANES]
            partner = jnp.where(first_half,
                                pltpu.roll(tj, LANES - HEAD_DIM // 2, axis=1),
                                pltpu.roll(tj, HEAD_DIM // 2, axis=1))
            outs.append(tj * cos + partner * sin)
        return jnp.concatenate(outs, axis=1) if len(outs) > 1 else outs[0]

    def deinterleave(val, ref1, ref4, ref16):
        ref1[0] = val.astype(BF16)
        nl = B_W // LANES
        for j in range(nl):
            tmp_ref[j] = val[:, j * LANES:(j + 1) * LANES]
        for d, ref in ((4, ref4), (16, ref16)):
            for r in range(d):
                parts = [tmp_ref[j, pl.ds(r, tm // d, stride=d), :] for j in range(nl)]
                ref[0, r] = jnp.concatenate(parts, axis=1).astype(BF16)

    for c0 in range(0, TSHIFT_W, 256):
        shift_ref[0, :, c0:c0 + 256] = proj(c0, c0 + 256)
    ag_ref[0] = _silu(proj(COL_AG, COL_AG + A_W))
    bg_ref[0] = _silu(proj(COL_BG, COL_BG + B_W))
    cg_ref[0] = _silu(proj(COL_CG, COL_CG + C_QW))
    scale = HEAD_DIM ** -0.5
    deinterleave(rope(proj(COL_BQ, COL_BQ + B_W)) * scale, bq1_ref, bq4_ref, bq16_ref)
    deinterleave(rope(proj(COL_BK, COL_BK + B_W)), bk1_ref, bk4_ref, bk16_ref)
    deinterleave(proj(COL_BV, COL_BV + B_W), bv1_ref, bv4_ref, bv16_ref)
    for c0 in range(0, C_QW, 256):
        cq_ref[0, :, c0:c0 + 256] = (rope(proj(COL_CQ + c0, COL_CQ + c0 + 256)) * scale).astype(BF16)
    ck_ref[0] = rope(proj(COL_CK, COL_CK + C_KVW)).astype(BF16)
    cv_ref[0] = proj(COL_CV, COL_CV + C_KVW).astype(BF16)


def _proj_call(x, g, w_bf16, cos_t, sin_t):
    b, t, _ = x.shape
    tm = ROW_TILE
    grid = (b, t // tm)

    def row(w):
        return pl.BlockSpec((1, tm, w), lambda bi, i: (bi, i, 0))

    def strided(d):
        return pl.BlockSpec((1, d, tm // d, B_W), lambda bi, i: (bi, 0, i, 0))

    def nat(w, dt):
        return jax.ShapeDtypeStruct((b, t, w), dt)

    def sshape(d):
        return jax.ShapeDtypeStruct((b, d, t // d, B_W), BF16)

    out_shape = [nat(TSHIFT_W, F32), nat(A_W, F32), nat(B_W, F32),
                 nat(C_QW, BF16), nat(C_KVW, BF16), nat(C_KVW, BF16), nat(C_QW, F32),
                 nat(B_W, BF16), nat(B_W, BF16), nat(B_W, BF16),
                 sshape(4), sshape(4), sshape(4), sshape(16), sshape(16), sshape(16)]
    out_specs = [row(TSHIFT_W), row(A_W), row(B_W),
                 row(C_QW), row(C_KVW), row(C_KVW), row(C_QW),
                 row(B_W), row(B_W), row(B_W),
                 strided(4), strided(4), strided(4), strided(16), strided(16), strided(16)]
    in_specs = [row(D_MODEL),
                pl.BlockSpec((1, D_MODEL), lambda bi, i: (0, 0)),
                pl.BlockSpec((D_MODEL, IN_W), lambda bi, i: (0, 0)),
                pl.BlockSpec((tm, LANES), lambda bi, i: (i, 0)),
                pl.BlockSpec((tm, LANES), lambda bi, i: (i, 0))]
    return pl.pallas_call(
        _proj_kernel, grid=grid, in_specs=in_specs, out_specs=out_specs, out_shape=out_shape,
        scratch_shapes=[pltpu.VMEM((B_W // LANES, tm, LANES), F32)],
        compiler_params=pltpu.CompilerParams(
            dimension_semantics=("arbitrary", "arbitrary"), vmem_limit_bytes=VMEM_LIMIT),
        name="proj",
    )(x, g, w_bf16, cos_t, sin_t)


def _prep_kernel(f_ref, fp_ref, fn_ref, mu_ref, w0_ref, w2_ref, a0_ref, a2_ref, kk_ref_, ka_ref, rk_ref, gsum_ref,
                 r_ref, v_ref, kkn_ref, bonus_ref, logw_ref, keff_ref, beta_ref):
    i = pl.program_id(1)
    n = pl.num_programs(1)
    tp = f_ref.shape[1]
    f = f_ref[0]
    row = lax.broadcasted_iota(jnp.int32, (tp, 1), 0)
    prev_row = jnp.where(i > 0, fp_ref[0, HALO_ROWS - 1:HALO_ROWS, :], 0.0)
    next_row = jnp.where(i < n - 1, fn_ref[0, 0:1, :], 0.0)
    prev = jnp.where(row == 0, prev_row, pltpu.roll(f, 1, axis=0))
    nxt = jnp.where(row == tp - 1, next_row, pltpu.roll(f, tp - 1, axis=0))
    s = f + mu_ref[...] * (0.5 * (prev + nxt) - f)

    r = s[:, 0:A_W]
    k = s[:, A_W:2 * A_W]
    v = s[:, 2 * A_W:3 * A_W]
    lora_in = s[:, 3 * A_W:TSHIFT_W]
    wl = _mm(jnp.tanh(lora_in), w2_ref[...], HIGHEST)
    al = _mm(lora_in, a2_ref[...], HIGHEST)
    gsum = gsum_ref[...]

    kk0 = k * kk_ref_[...]
    ss = _mm(kk0 * kk0, gsum, HIGHEST)
    kkn = kk0 / jnp.maximum(jnp.sqrt(ss), 1e-12)
    r_ref[0] = r
    v_ref[0] = v
    kkn_ref[0] = kkn
    keff_sum = jnp.zeros_like(k)
    for e in range(2):
        w = -_softplus(-(w0_ref[e:e + 1, :] + wl[:, e * A_W:(e + 1) * A_W])) - 0.5
        logw_ref[e, 0] = -jnp.exp(w)
        a = _sigmoid(a0_ref[e:e + 1, :] + al[:, e * A_W:(e + 1) * A_W])
        keff = k * (1.0 + (a - 1.0) * ka_ref[...])
        keff_ref[e, 0] = keff
        beta_ref[e, 0] = a * kkn
        keff_sum = keff_sum + keff
    bonus_ref[0] = _mm(r * keff_sum * rk_ref[...], gsum, HIGHEST) * v


def _prep_call(shift, mu, w0, w2p, a0, a2p, k_k, k_a, r_k, gsum):
    b, t, _ = shift.shape
    tp = ROW_TILE
    nb = tp // HALO_ROWS
    last = t // HALO_ROWS - 1
    grid = (b, t // tp)

    def const(shape):
        return pl.BlockSpec(shape, lambda bi, i: (0,) * len(shape))

    in_specs = [pl.BlockSpec((1, tp, TSHIFT_W), lambda bi, i: (bi, i, 0)),
                pl.BlockSpec((1, HALO_ROWS, TSHIFT_W), lambda bi, i: (bi, jnp.maximum(i * nb - 1, 0), 0)),
                pl.BlockSpec((1, HALO_ROWS, TSHIFT_W), lambda bi, i: (bi, jnp.minimum((i + 1) * nb, last), 0)),
                const((1, TSHIFT_W)), const((2, A_W)), const((4 * LORA, 2 * A_W)),
                const((2, A_W)), const((4 * LORA, 2 * A_W)),
                const((1, A_W)), const((1, A_W)), const((1, A_W)), const((A_W, A_W))]
    tok = pl.BlockSpec((1, tp, A_W), lambda bi, i: (bi, i, 0))
    both = pl.BlockSpec((2, 1, tp, A_W), lambda bi, i: (0, bi, i, 0))
    tok_shape = jax.ShapeDtypeStruct((b, t, A_W), F32)
    both_shape = jax.ShapeDtypeStruct((2, b, t, A_W), F32)
    return pl.pallas_call(
        _prep_kernel, grid=grid, in_specs=in_specs,
        out_specs=[tok, tok, tok, tok, both, both, both],
        out_shape=[tok_shape] * 4 + [both_shape] * 3,
        compiler_params=pltpu.CompilerParams(
            dimension_semantics=("arbitrary", "arbitrary"), vmem_limit_bytes=VMEM_LIMIT),
        name="prep",
    )(shift, shift, shift, mu, w0, w2p, a0, a2p, k_k, k_a, r_k, gsum)


def _split_bf16(x, passes):
    hi = x.astype(BF16)
    if passes == 1:
        return (hi,)
    return (hi, (x - hi.astype(F32)).astype(BF16))


def _mmx(a, b, dims, passes, expand=None):
    ap = _split_bf16(a, passes)
    bp = _split_bf16(b, passes)
    if expand is not None:
        bp = tuple(expand(part) for part in bp)
    out = _dot(ap[0], bp[0], dims)
    if passes == 3:
        out = (_dot(ap[1], bp[0], dims) + _dot(ap[0], bp[1], dims)) + out
    return out


_NN = ((1,), (0,))
_NT = ((1,), (1,))
_TN = ((0,), (0,))
WKV_PASSES_SCORES = 1
WKV_PASSES_INVERSE = 1
WKV_PASSES_STATE = 1
WKV_PASSES_VALUES = 1
WKV_INV_BASE = 16


def _wkv_kernel(rf_ref, vf_ref, kkf_ref, lwf_ref, kef_ref, bef_ref,
                rb_ref, vb_ref, kkb_ref, lwb_ref, keb_ref, beb_ref,
                yf_ref, yb_ref, ht_ref):
    c = WKV_CHUNK
    nchunks = rf_ref.shape[1] // c
    nh = A_HEADS

    @pl.when(pl.program_id(1) == 0)
    def _():
        ht_ref[...] = jnp.zeros_like(ht_ref)

    row = lax.broadcasted_iota(jnp.int32, (c, c), 0)
    col = lax.broadcasted_iota(jnp.int32, (c, c), 1)
    row_w = lax.broadcasted_iota(jnp.int32, (c, A_W), 0)
    col_w = lax.broadcasted_iota(jnp.int32, (c, A_W), 1) % c
    blk_r = lax.broadcasted_iota(jnp.int32, (A_W, A_W), 0) // HEAD_DIM
    blk_c = lax.broadcasted_iota(jnp.int32, (A_W, A_W), 1) // HEAD_DIM
    same_head = blk_r == blk_c
    eye_w = (row_w == col_w).astype(F32)

    def blockdiag(x):
        return jnp.where(same_head, jnp.concatenate([x] * nh, axis=0), jnp.zeros((), x.dtype))

    def unit_triangular_inverse(a_kb):
        base = WKV_INV_BASE
        same_base = (row_w // base) == (col_w // base)
        p = jnp.where(same_base, -a_kb, 0.0)
        tinv = eye_w + p
        p = _mmx(p, p, _NN, WKV_PASSES_INVERSE, blockdiag)
        yield
        span = 2
        while span < base:
            last = 2 * span >= base
            both = _mmx(tinv if last else jnp.concatenate([tinv, p], axis=0), p, _NN, WKV_PASSES_INVERSE, blockdiag)
            yield
            tinv = tinv + both[:c]
            if not last:
                p = both[c:]
            span *= 2
        size = base
        while size < c:
            coupling = ((row_w // (2 * size)) == (col_w // (2 * size))) & ((row_w // size) != (col_w // size))
            w = _mmx(jnp.where(coupling, a_kb, 0.0), tinv, _NN, WKV_PASSES_INVERSE, blockdiag)
            yield
            tinv = tinv - _mmx(tinv, w, _NN, WKV_PASSES_INVERSE, blockdiag)
            yield
            size *= 2
        return tinv

    def one_chunk(rev, start, r_ref, v_ref, kk_ref, lw_ref, ke_ref, be_ref, y_ref, ht_box, slot):
        rows = pl.ds(start, c)
        incl = (col >= row) if rev else (col <= row)
        incl_w = (col_w >= row_w) if rev else (col_w <= row_w)
        strict_w = (col_w > row_w) if rev else (col_w < row_w)

        lw = lw_ref[0, 0, rows, :]
        cum = _mm(incl.astype(F32), lw, HIGHEST)
        g_tot = jnp.exp(jnp.sum(lw, axis=0, keepdims=True))
        g_inv = jnp.exp(-cum)
        v = v_ref[0, rows, :]
        qt = r_ref[0, rows, :] * jnp.exp(cum)
        kt = kk_ref[0, rows, :] * jnp.exp(cum - lw)
        kb = ke_ref[0, 0, rows, :] * g_inv
        bb = be_ref[0, 0, rows, :] * g_inv
        kq = jnp.concatenate([kt, qt], axis=0)

        a_k = _mmx(kq, kb, _NT, WKV_PASSES_SCORES, blockdiag)
        yield
        a_b = _mmx(kq, bb, _NT, WKV_PASSES_SCORES, blockdiag)
        yield
        a_kk = jnp.where(strict_w, a_k[:c], 0.0)
        a_rk = jnp.where(incl_w, a_k[c:], 0.0)
        a_kb = jnp.where(strict_w, a_b[:c], 0.0)
        a_rb = jnp.where(incl_w, a_b[c:], 0.0)
        av = _mmx(jnp.concatenate([a_kk, a_rk], axis=0), v, _NN, WKV_PASSES_VALUES, blockdiag)
        yield
        tinv = yield from unit_triangular_inverse(a_kb)

        while ht_box[slot] is None:
            yield
        ht = ht_box[slot]
        hq = _mmx(kq, ht, _NT, WKV_PASSES_STATE)
        yield
        u = _mmx(tinv, hq[:c] + av[:c], _NN, WKV_PASSES_VALUES, blockdiag)
        yield
        upd = _mmx(jnp.concatenate([v, u], axis=0),
                   jnp.concatenate([kb * g_tot, -(bb * g_tot)], axis=0), _TN, WKV_PASSES_STATE)
        ht_box[slot + 1] = ht * g_tot + jnp.where(same_head, upd, 0.0)
        yield
        y_ref[0, rows, :] = hq[c:] + av[c:] - _mmx(a_rb, u, _NN, WKV_PASSES_VALUES, blockdiag)

    boxes = [[ht_ref[0]] + [None] * nchunks, [ht_ref[1]] + [None] * nchunks]
    tasks = []
    for ci in range(nchunks):
        tasks.append(one_chunk(False, ci * c, rf_ref, vf_ref, kkf_ref, lwf_ref, kef_ref, bef_ref, yf_ref,
                               boxes[0], ci))
        tasks.append(one_chunk(True, (nchunks - 1 - ci) * c, rb_ref, vb_ref, kkb_ref, lwb_ref, keb_ref, beb_ref,
                               yb_ref, boxes[1], ci))
    while tasks:
        alive = []
        for task in tasks:
            try:
                next(task)
                alive.append(task)
            except StopIteration:
                pass
        tasks = alive
    ht_ref[0] = boxes[0][nchunks]
    ht_ref[1] = boxes[1][nchunks]


def _wkv_call(r, v, kk, logw, keff, beta):
    b, t, _ = r.shape
    tb = WKV_ROWS
    nblk = t // tb
    grid = (b, nblk)
    tok_f = pl.BlockSpec((1, tb, A_W), lambda bi, j: (bi, j, 0))
    tok_b = pl.BlockSpec((1, tb, A_W), lambda bi, j: (bi, nblk - 1 - j, 0))
    dir_f = pl.BlockSpec((1, 1, tb, A_W), lambda bi, j: (0, bi, j, 0))
    dir_b = pl.BlockSpec((1, 1, tb, A_W), lambda bi, j: (1, bi, nblk - 1 - j, 0))
    out = jax.ShapeDtypeStruct((b, t, A_W), F32)
    return pl.pallas_call(
        _wkv_kernel, grid=grid,
        in_specs=[tok_f, tok_f, tok_f, dir_f, dir_f, dir_f, tok_b, tok_b, tok_b, dir_b, dir_b, dir_b],
        out_specs=[tok_f, tok_b], out_shape=[out, out],
        scratch_shapes=[pltpu.VMEM((2, A_W, A_W), F32)],
        compiler_params=pltpu.CompilerParams(
            dimension_semantics=("arbitrary", "arbitrary"), vmem_limit_bytes=VMEM_LIMIT),
        name="wkv",
    )(r, v, kk, logw, keff, beta, r, v, kk, logw, keff, beta)


def _band_kernel(*refs, radius, hq, hkv, seq_len, has_sink, want_lse, qblk, group):
    refs = list(refs)
    sink_ref = refs.pop(0) if has_sink else None
    q_ref, kp_ref, km_ref, kn_ref, vp_ref, vm_ref, vn_ref = refs[:7]
    o_ref = refs[7]
    lse_ref = refs[8] if want_lse else None
    kt_ref, vt_ref, ot_ref, m_ref, d_ref = refs[-5:]
    rad = radius
    tq = q_ref.shape[1]
    grp = hq // hkv
    win = qblk + 2 * rad
    nq = tq // qblk
    i = pl.program_id(1)

    def transposed(prev, main, nxt):
        cat = jnp.concatenate([prev[0], main[0], nxt[0]], axis=0)
        return cat.astype(F32).T.astype(BF16)

    kt_ref[...] = transposed(kp_ref, km_ref, kn_ref)
    vt_ref[...] = transposed(vp_ref, vm_ref, vn_ref)

    a = lax.broadcasted_iota(jnp.int32, (qblk, win), 0)
    cc = lax.broadcasted_iota(jnp.int32, (qblk, win), 1)
    band_bias = jnp.where(jnp.abs(cc - rad - a) <= rad, 0.0, NEG_INF)

    def bias_for(j):
        bias = band_bias
        kpos = i * tq + j * qblk - rad + cc
        if j == 0:
            bias = bias + jnp.where(kpos >= 0, 0.0, NEG_INF)
        if j == nq - 1:
            bias = bias + jnp.where(kpos < seq_len, 0.0, NEG_INF)
        return bias

    def scores(j, g, hh, bias):
        h = g * grp + hh
        q = q_ref[0, j * qblk:(j + 1) * qblk, h * HEAD_DIM:(h + 1) * HEAD_DIM]
        kwin = kt_ref[g * HEAD_DIM:(g + 1) * HEAD_DIM, j * qblk:j * qblk + win]
        s = _mm(q, kwin) + bias
        m = jnp.max(s, axis=-1, keepdims=True)
        if has_sink:
            m = jnp.maximum(m, sink_ref[h])
        p = jnp.exp(s - m)
        denom = jnp.sum(p, axis=-1, keepdims=True)
        if has_sink:
            denom = denom + jnp.exp(sink_ref[h] - m)
        return p.astype(BF16), m, denom

    def values(j, g, hh, p, m, denom):
        h = g * grp + hh
        qrows = slice(j * qblk, (j + 1) * qblk)
        hcols = slice(h * HEAD_DIM, (h + 1) * HEAD_DIM)
        vwin = vt_ref[g * HEAD_DIM:(g + 1) * HEAD_DIM, j * qblk:j * qblk + win]
        ot_ref[hcols, qrows] = _mm_nt(vwin, p)
        d_ref[qrows, hcols] = jnp.broadcast_to(denom, (qblk, HEAD_DIM))
        if want_lse:
            m_ref[qrows, hcols] = jnp.broadcast_to(m, (qblk, HEAD_DIM))

    bodies = [(j, g, hh) for j in range(nq) for g in range(hkv) for hh in range(grp)]
    groups = [bodies[n:n + group] for n in range(0, len(bodies), group)]
    biases = {}

    def run_scores(grp_bodies):
        out = []
        for (j, g, hh) in grp_bodies:
            if j not in biases:
                biases[j] = bias_for(j)
            out.append(scores(j, g, hh, biases[j]))
        return out

    pending = run_scores(groups[0])
    for n, grp_bodies in enumerate(groups):
        nxt = run_scores(groups[n + 1]) if n + 1 < len(groups) else None
        for body, state in zip(grp_bodies, pending):
            values(*body, *state)
        pending = nxt

    denom = d_ref[...]
    o_ref[0] = ot_ref[...].T / denom
    if want_lse:
        lse_ref[0] = m_ref[...] + jnp.log(denom)


def _band_call(q, k, v, *, radius, hq, hkv, sink=None, want_lse=False, name):
    s, l, wq = q.shape
    wk = k.shape[2]
    tq = min(ROW_TILE, l)
    nb = tq // radius
    last = l // radius - 1
    grid = (s, l // tq)
    main_q = pl.BlockSpec((1, tq, wq), lambda si, i: (si, i, 0))
    main_k = pl.BlockSpec((1, tq, wk), lambda si, i: (si, i, 0))
    prev_k = pl.BlockSpec((1, radius, wk), lambda si, i: (si, jnp.maximum(i * nb - 1, 0), 0))
    next_k = pl.BlockSpec((1, radius, wk), lambda si, i: (si, jnp.minimum((i + 1) * nb, last), 0))
    in_specs = [main_q, prev_k, main_k, next_k, prev_k, main_k, next_k]
    args = [q, k, k, k, v, v, v]
    if sink is not None:
        in_specs = [pl.BlockSpec(memory_space=pltpu.SMEM)] + in_specs
        args = [sink] + args
    out_specs = [main_q]
    out_shape = [jax.ShapeDtypeStruct((s, l, wq), F32)]
    if want_lse:
        out_specs.append(main_q)
        out_shape.append(jax.ShapeDtypeStruct((s, l, wq), F32))
    group = max(1, BAND_GROUP_ROWS // BAND_QBLK)
    kern = functools.partial(_band_kernel, radius=radius, hq=hq, hkv=hkv, seq_len=l,
                             has_sink=sink is not None, want_lse=want_lse, qblk=BAND_QBLK, group=group)
    return pl.pallas_call(
        kern, grid=grid, in_specs=in_specs, out_specs=out_specs, out_shape=out_shape,
        scratch_shapes=[pltpu.VMEM((wk, tq + 2 * radius), BF16), pltpu.VMEM((wk, tq + 2 * radius), BF16),
                        pltpu.VMEM((wq, tq), F32), pltpu.VMEM((tq, wq), F32), pltpu.VMEM((tq, wq), F32)],
        compiler_params=pltpu.CompilerParams(
            dimension_semantics=("arbitrary", "arbitrary"), vmem_limit_bytes=VMEM_LIMIT),
        name=name,
    )(*args)


def _post_kernel(*refs, final):
    refs = list(refs)
    (x_ref, yf_ref, yb_ref, bonus_ref, ag_ref, o1_ref, l1_ref, o4_ref, l4_ref, o16_ref, l16_ref,
     bg_ref, co_ref, cg_ref, lnw_ref, lnb_ref, gavg_ref, wo_ref) = refs[:18]
    fg_ref = refs[18] if final else None
    out_ref = refs[-5]
    so4, sl4, so16, sl16 = refs[-4:]
    tm = x_ref.shape[1]

    gavg = gavg_ref[...]
    ya = yf_ref[0] + yb_ref[0]
    mu = _mm(ya, gavg, HIGHEST)
    dev = ya - mu
    var = _mm(dev * dev, gavg, HIGHEST)
    yn = dev * lax.rsqrt(var + HEADNORM_EPS) * lnw_ref[...] + lnb_ref[...]
    mix_a = (yn + bonus_ref[0]) * ag_ref[0]

    nl = B_W // LANES

    def interleave(src_ref, dst, d):
        for r in range(d):
            for j in range(nl):
                dst[j, pl.ds(r, tm // d, stride=d), :] = src_ref[0, r, :, j * LANES:(j + 1) * LANES]
        return jnp.concatenate([dst[j] for j in range(nl)], axis=1)

    o4 = interleave(o4_ref, so4, 4)
    l4 = interleave(l4_ref, sl4, 4)
    o16 = interleave(o16_ref, so16, 16)
    l16 = interleave(l16_ref, sl16, 16)
    l1 = l1_ref[0]
    lmax = jnp.maximum(jnp.maximum(l1, l4), l16)
    w1 = jnp.exp(l1 - lmax)
    w4 = jnp.exp(l4 - lmax)
    w16 = jnp.exp(l16 - lmax)
    mix_b = (w1 * o1_ref[0] + w4 * o4 + w16 * o16) / (w1 + w4 + w16) * bg_ref[0]

    mix_c = co_ref[0] * cg_ref[0]
    xn = (x_ref[0]
          + _mm(mix_a.astype(BF16), wo_ref[0:A_W, :])
          + _mm(mix_b.astype(BF16), wo_ref[A_W:A_W + B_W, :])
          + _mm(mix_c.astype(BF16), wo_ref[A_W + B_W:, :]))
    if final:
        xn = xn * lax.rsqrt(jnp.mean(xn * xn, axis=-1, keepdims=True) + RMS_EPS) * fg_ref[...]
    out_ref[0] = xn


def _post_call(x, yf, yb, bonus, ag, o1, l1, o4, l4, o16, l16, bg, co, cg, lnw, lnb, gavg, wo_bf16, final_g):
    b, t, _ = x.shape
    tm = ROW_TILE
    grid = (b, t // tm)
    final = final_g is not None

    def row(w):
        return pl.BlockSpec((1, tm, w), lambda bi, i: (bi, i, 0))

    def strided(d):
        return pl.BlockSpec((1, d, tm // d, B_W), lambda bi, i: (bi, 0, i, 0))

    def const(shape):
        return pl.BlockSpec(shape, lambda bi, i: (0,) * len(shape))

    in_specs = [row(D_MODEL), row(A_W), row(A_W),
                row(A_W), row(A_W), row(B_W), row(B_W), strided(4), strided(4), strided(16), strided(16),
                row(B_W), row(C_QW), row(C_QW),
                const((1, A_W)), const((1, A_W)), const((A_W, A_W)), const((D_MODEL, D_MODEL))]
    args = [x, yf, yb, bonus, ag, o1, l1, o4, l4, o16, l16, bg, co, cg, lnw, lnb, gavg, wo_bf16]
    if final:
        in_specs.append(const((1, D_MODEL)))
        args.append(final_g)
    return pl.pallas_call(
        functools.partial(_post_kernel, final=final), grid=grid, in_specs=in_specs,
        out_specs=row(D_MODEL), out_shape=jax.ShapeDtypeStruct((b, t, D_MODEL), F32),
        scratch_shapes=[pltpu.VMEM((B_W // LANES, tm, LANES), F32)] * 4,
        compiler_params=pltpu.CompilerParams(
            dimension_semantics=("arbitrary", "arbitrary"), vmem_limit_bytes=VMEM_LIMIT),
        name="post",
    )(*args)


def _rope_tables(t):
    inv = ROPE_THETA ** (-jnp.arange(0, HEAD_DIM, 2, dtype=F32) / HEAD_DIM)
    ang = jnp.arange(t, dtype=F32)[:, None] * inv[None, :]
    cos = jnp.cos(ang)
    sin = jnp.sin(ang)
    reps = LANES // HEAD_DIM
    cos_t = jnp.tile(jnp.concatenate([cos, cos], axis=1), (1, reps))
    sin_t = jnp.tile(jnp.concatenate([-sin, sin], axis=1), (1, reps))
    return cos_t, sin_t


def _head_block_matrix(value):
    idx = np.arange(A_W) // HEAD_DIM
    return jnp.asarray((idx[:, None] == idx[None, :]).astype(np.float32) * value)


def _lora_weights(w2):
    z = jnp.zeros((LORA, A_W), F32)
    top = jnp.concatenate([jnp.concatenate([w2[0], z], axis=1), jnp.concatenate([z, w2[1]], axis=1)], axis=0)
    pad = jnp.zeros((2 * LORA, 2 * A_W), F32)
    return top, pad


def _layer(x, p, tables, gsum, gavg, final_g):
    b, t, _ = x.shape
    cos_t, sin_t = tables
    (shift, ag, bg, cq, ck, cv, cg,
     bq1, bk1, bv1, bq4, bk4, bv4, bq16, bk16, bv16) = _proj_call(x, p["norm_g"], p["w_in"], cos_t, sin_t)

    r, v, kk, bonus, logw, keff, beta = _prep_call(
        shift, p["mu"], p["w0"], p["w2p"], p["a0"], p["a2p"], p["k_k"], p["k_a"], p["r_k"], gsum)
    yf, yb = _wkv_call(r, v, kk, logw, keff, beta)

    branch = []
    for (window, d), (q, k, vv) in zip(DILATED_PAIRS, ((bq1, bk1, bv1), (bq4, bk4, bv4), (bq16, bk16, bv16))):
        l = t // d
        o, lse = _band_call(q.reshape(b * d, l, B_W), k.reshape(b * d, l, B_W), vv.reshape(b * d, l, B_W),
                            radius=window // (2 * d), hq=B_HEADS, hkv=B_HEADS, want_lse=True, name=f"dil{d}")
        branch += [o.reshape(b, d, l, B_W) if d > 1 else o, lse.reshape(b, d, l, B_W) if d > 1 else lse]
    (co,) = _band_call(cq, ck, cv, radius=C_RADIUS, hq=C_HEADS, hkv=C_KV_HEADS, sink=p["sink"], name="win")

    return _post_call(x, yf, yb, bonus, ag, *branch, bg, co, cg, p["ln_w"], p["ln_b"], gavg, p["w_out"], final_g)


def _trunk(x, layers, final_g, gsum, gavg):
    tables = _rope_tables(x.shape[1])
    for li, p in enumerate(layers):
        x = _layer(x, p, tables, gsum, gavg, final_g if li == len(layers) - 1 else None)
    return x


def kernel(x_prompt, x_sample, norm_g, w_in, tshift_mu, rwkv_w0, rwkv_w2, rwkv_a0, rwkv_a2, rwkv_k_k, rwkv_k_a,
           rwkv_r_k, ln_x_w, ln_x_b, attn_sink, w_out, final_g):
    depth = norm_g.shape[0]
    layers = []
    for l in range(depth):
        w2_top, pad = _lora_weights(rwkv_w2[l])
        a2_top, _ = _lora_weights(rwkv_a2[l])
        layers.append(dict(
            norm_g=norm_g[l][None, :], w_in=w_in[l].astype(BF16), mu=tshift_mu[l][None, :],
            w0=rwkv_w0[l], w2p=jnp.concatenate([w2_top, pad], axis=0),
            a0=rwkv_a0[l], a2p=jnp.concatenate([pad, a2_top], axis=0),
            k_k=rwkv_k_k[l][None, :], k_a=rwkv_k_a[l][None, :], r_k=rwkv_r_k[l].reshape(1, A_W),
            ln_w=ln_x_w[l][None, :], ln_b=ln_x_b[l][None, :], sink=attn_sink[l],
            w_out=w_out[l].astype(BF16)))
    gsum = _head_block_matrix(1.0)
    gavg = _head_block_matrix(1.0 / HEAD_DIM)
    fg = final_g[None, :]
    return (_trunk(x_prompt, layers, fg, gsum, gavg), _trunk(x_sample, layers, fg, gsum, gavg))
```

```python
import functools
import math

import jax
import jax.numpy as jnp
import numpy as np
from jax import lax
from jax.experimental import pallas as pl
from jax.experimental.pallas import tpu as pltpu

F32 = jnp.float32
BF16 = jnp.bfloat16

D_MODEL = 1024
HEAD_DIM = 64
A_HEADS = 4
A_W = A_HEADS * HEAD_DIM
LORA = 64
B_HEADS = 4
B_W = B_HEADS * HEAD_DIM
DILATED_PAIRS = ((128, 1), (512, 4), (2048, 16))
C_HEADS = 8
C_KV_HEADS = 2
C_QW = C_HEADS * HEAD_DIM
C_KVW = C_KV_HEADS * HEAD_DIM
C_RADIUS = 128
ROPE_THETA = 10000.0
RMS_EPS = 1e-5
HEADNORM_EPS = 64e-5
NEG_INF = -1e30
TSHIFT_W = 3 * A_W + 4 * LORA
IN_W = 3584

COL_AG = TSHIFT_W
COL_BQ = COL_AG + A_W
COL_BK = COL_BQ + B_W
COL_BV = COL_BK + B_W
COL_BG = COL_BV + B_W
COL_CQ = COL_BG + B_W
COL_CK = COL_CQ + C_QW
COL_CV = COL_CK + C_KVW
COL_CG = COL_CV + C_KVW

LANES = 128
ROW_TILE = 512
WKV_CHUNK = 64
WKV_ROWS = 256
WKV_SEQS = 2
HALO_ROWS = 8
BAND_GROUP_ROWS = 512
BAND_QBLK = 128
VMEM_LIMIT = 56 * 1024 * 1024


def _dot(a, b, dims):
    return lax.dot_general(a, b, (dims, ((), ())), preferred_element_type=F32)


def _mm(a, b):
    return _dot(a, b, ((1,), (0,)))


def _mm_nt(a, b):
    return _dot(a, b, ((1,), (1,)))


def _bf16_terms(x, n):
    terms = []
    for _ in range(n):
        t = x.astype(BF16)
        terms.append(t)
        x = x - t.astype(F32)
    return terms


def _mm_split_lhs(a, b_bf16, n=3):
    parts = _bf16_terms(a, n)
    out = _mm(parts[-1], b_bf16)
    for t in reversed(parts[:-1]):
        out = out + _mm(t, b_bf16)
    return out


def _mm_split_rhs(a_bf16, b, n=3):
    parts = _bf16_terms(b, n)
    out = _mm(a_bf16, parts[-1])
    for t in reversed(parts[:-1]):
        out = out + _mm(a_bf16, t)
    return out


def _mm_bf16x3(a, b):
    ah, al = _bf16_terms(a, 2)
    bh, bl = _bf16_terms(b, 2)
    return (_mm(al, bh) + _mm(ah, bl)) + _mm(ah, bh)


def _sigmoid(x):
    return 1.0 / (1.0 + jnp.exp(-x))


def _silu(x):
    return x * _sigmoid(x)


def _softplus(x):
    return jnp.maximum(x, 0.0) + jnp.log(1.0 + jnp.exp(-jnp.abs(x)))


def _proj_kernel(x_ref, g_ref, w_ref, cos_ref, sin_ref,
                 shift_ref, ag_ref, bg_ref, cq_ref, ck_ref, cv_ref, cg_ref,
                 bq1_ref, bk1_ref, bv1_ref, bq4_ref, bk4_ref, bv4_ref, bq16_ref, bk16_ref, bv16_ref,
                 tmp_ref):
    tm = x_ref.shape[1]
    x = x_ref[0]
    h = x * lax.rsqrt(jnp.mean(x * x, axis=-1, keepdims=True) + RMS_EPS) * g_ref[...]
    hb = h.astype(BF16)

    def proj(c0, c1):
        return _mm(hb, w_ref[:, c0:c1])

    cos = cos_ref[...]
    sin = sin_ref[...]
    lane = lax.broadcasted_iota(jnp.int32, (tm, LANES), 1)
    first_half = (lane % HEAD_DIM) < (HEAD_DIM // 2)

    def rope(t):
        outs = []
        for j in range(t.shape[1] // LANES):
            tj = t[:, j * LANES:(j + 1) * LANES]
            partner = jnp.where(first_half,
                                pltpu.roll(tj, LANES - HEAD_DIM // 2, axis=1),
                                pltpu.roll(tj, HEAD_DIM // 2, axis=1))
            outs.append(tj * cos + partner * sin)
        return jnp.concatenate(outs, axis=1) if len(outs) > 1 else outs[0]

    def deinterleave(val, ref1, ref4, ref16):
        ref1[0] = val.astype(BF16)
        nl = B_W // LANES
        for j in range(nl):
            tmp_ref[j] = val[:, j * LANES:(j + 1) * LANES]
        for d, ref in ((4, ref4), (16, ref16)):
            for r in range(d):
                parts = [tmp_ref[j, pl.ds(r, tm // d, stride=d), :] for j in range(nl)]
                ref[0, r] = jnp.concatenate(parts, axis=1).astype(BF16)

    for c0 in range(0, TSHIFT_W, 256):
        shift_ref[0, :, c0:c0 + 256] = proj(c0, c0 + 256)
    ag_ref[0] = _silu(proj(COL_AG, COL_AG + A_W)).astype(BF16)
    bg_ref[0] = _silu(proj(COL_BG, COL_BG + B_W)).astype(BF16)
    cg_ref[0] = _silu(proj(COL_CG, COL_CG + C_QW)).astype(BF16)
    scale = HEAD_DIM ** -0.5
    deinterleave(rope(proj(COL_BQ, COL_BQ + B_W)) * scale, bq1_ref, bq4_ref, bq16_ref)
    deinterleave(rope(proj(COL_BK, COL_BK + B_W)), bk1_ref, bk4_ref, bk16_ref)
    deinterleave(proj(COL_BV, COL_BV + B_W), bv1_ref, bv4_ref, bv16_ref)
    for c0 in range(0, C_QW, 256):
        cq_ref[0, :, c0:c0 + 256] = (rope(proj(COL_CQ + c0, COL_CQ + c0 + 256)) * scale).astype(BF16)
    ck_ref[0] = rope(proj(COL_CK, COL_CK + C_KVW)).astype(BF16)
    cv_ref[0] = proj(COL_CV, COL_CV + C_KVW).astype(BF16)


def _proj_call(x, g, w_bf16, cos_t, sin_t):
    b, t, _ = x.shape
    tm = ROW_TILE
    grid = (b, t // tm)

    def row(w):
        return pl.BlockSpec((1, tm, w), lambda bi, i: (bi, i, 0))

    def strided(d):
        return pl.BlockSpec((1, d, tm // d, B_W), lambda bi, i: (bi, 0, i, 0))

    def nat(w, dt):
        return jax.ShapeDtypeStruct((b, t, w), dt)

    def sshape(d):
        return jax.ShapeDtypeStruct((b, d, t // d, B_W), BF16)

    out_shape = [nat(TSHIFT_W, F32), nat(A_W, BF16), nat(B_W, BF16),
                 nat(C_QW, BF16), nat(C_KVW, BF16), nat(C_KVW, BF16), nat(C_QW, BF16),
                 nat(B_W, BF16), nat(B_W, BF16), nat(B_W, BF16),
                 sshape(4), sshape(4), sshape(4), sshape(16), sshape(16), sshape(16)]
    out_specs = [row(TSHIFT_W), row(A_W), row(B_W),
                 row(C_QW), row(C_KVW), row(C_KVW), row(C_QW),
                 row(B_W), row(B_W), row(B_W),
                 strided(4), strided(4), strided(4), strided(16), strided(16), strided(16)]
    in_specs = [row(D_MODEL),
                pl.BlockSpec((1, D_MODEL), lambda bi, i: (0, 0)),
                pl.BlockSpec((D_MODEL, IN_W), lambda bi, i: (0, 0)),
                pl.BlockSpec((tm, LANES), lambda bi, i: (i, 0)),
                pl.BlockSpec((tm, LANES), lambda bi, i: (i, 0))]
    return pl.pallas_call(
        _proj_kernel, grid=grid, in_specs=in_specs, out_specs=out_specs, out_shape=out_shape,
        scratch_shapes=[pltpu.VMEM((B_W // LANES, tm, LANES), F32)],
        compiler_params=pltpu.CompilerParams(
            dimension_semantics=("arbitrary", "arbitrary"), vmem_limit_bytes=VMEM_LIMIT),
        name="proj",
    )(x, g, w_bf16, cos_t, sin_t)


def _prep_kernel(f_ref, fp_ref, fn_ref, mu_ref, w0_ref, w2_ref, a0_ref, a2_ref, kk_ref_, ka_ref, rk_ref, gsum_ref,
                 r_ref, v_ref, kkn_ref, bonus_ref, logw_ref, keff_ref, beta_ref):
    i = pl.program_id(1)
    n = pl.num_programs(1)
    tp = f_ref.shape[1]
    f = f_ref[0]
    row = lax.broadcasted_iota(jnp.int32, (tp, 1), 0)
    prev_row = jnp.where(i > 0, fp_ref[0, HALO_ROWS - 1:HALO_ROWS, :], 0.0)
    next_row = jnp.where(i < n - 1, fn_ref[0, 0:1, :], 0.0)
    prev = jnp.where(row == 0, prev_row, pltpu.roll(f, 1, axis=0))
    nxt = jnp.where(row == tp - 1, next_row, pltpu.roll(f, tp - 1, axis=0))
    s = f + mu_ref[...] * (0.5 * (prev + nxt) - f)

    r = s[:, 0:A_W]
    k = s[:, A_W:2 * A_W]
    v = s[:, 2 * A_W:3 * A_W]
    lora_in = s[:, 3 * A_W:TSHIFT_W]
    wl = _mm_bf16x3(jnp.tanh(lora_in), w2_ref[...])
    al = _mm_bf16x3(lora_in, a2_ref[...])
    gsum = gsum_ref[...]

    kk0 = k * kk_ref_[...]
    ss = _mm_split_lhs(kk0 * kk0, gsum)
    kkn = kk0 / jnp.maximum(jnp.sqrt(ss), 1e-12)
    r_ref[0] = r
    v_ref[0] = v
    kkn_ref[0] = kkn
    keff_sum = jnp.zeros_like(k)
    for e in range(2):
        w = -_softplus(-(w0_ref[e:e + 1, :] + wl[:, e * A_W:(e + 1) * A_W])) - 0.5
        logw_ref[e, 0] = -jnp.exp(w)
        a = _sigmoid(a0_ref[e:e + 1, :] + al[:, e * A_W:(e + 1) * A_W])
        keff = k * (1.0 + (a - 1.0) * ka_ref[...])
        keff_ref[e, 0] = keff
        beta_ref[e, 0] = a * kkn
        keff_sum = keff_sum + keff
    bonus_ref[0] = _mm_split_lhs(r * keff_sum * rk_ref[...], gsum) * v


def _prep_call(shift, mu, w0, w2p, a0, a2p, k_k, k_a, r_k, gsum):
    b, t, _ = shift.shape
    tp = ROW_TILE
    nb = tp // HALO_ROWS
    last = t // HALO_ROWS - 1
    grid = (b, t // tp)

    def const(shape):
        return pl.BlockSpec(shape, lambda bi, i: (0,) * len(shape))

    in_specs = [pl.BlockSpec((1, tp, TSHIFT_W), lambda bi, i: (bi, i, 0)),
                pl.BlockSpec((1, HALO_ROWS, TSHIFT_W), lambda bi, i: (bi, jnp.maximum(i * nb - 1, 0), 0)),
                pl.BlockSpec((1, HALO_ROWS, TSHIFT_W), lambda bi, i: (bi, jnp.minimum((i + 1) * nb, last), 0)),
                const((1, TSHIFT_W)), const((2, A_W)), const((4 * LORA, 2 * A_W)),
                const((2, A_W)), const((4 * LORA, 2 * A_W)),
                const((1, A_W)), const((1, A_W)), const((1, A_W)), const((A_W, A_W))]
    tok = pl.BlockSpec((1, tp, A_W), lambda bi, i: (bi, i, 0))
    both = pl.BlockSpec((2, 1, tp, A_W), lambda bi, i: (0, bi, i, 0))
    tok_shape = jax.ShapeDtypeStruct((b, t, A_W), F32)
    both_shape = jax.ShapeDtypeStruct((2, b, t, A_W), F32)
    return pl.pallas_call(
        _prep_kernel, grid=grid, in_specs=in_specs,
        out_specs=[tok, tok, tok, tok, both, both, both],
        out_shape=[tok_shape] * 4 + [both_shape] * 3,
        compiler_params=pltpu.CompilerParams(
            dimension_semantics=("arbitrary", "arbitrary"), vmem_limit_bytes=VMEM_LIMIT),
        name="prep",
    )(shift, shift, shift, mu, w0, w2p, a0, a2p, k_k, k_a, r_k, gsum)


def _split_bf16(x, passes):
    hi = x.astype(BF16)
    if passes == 1:
        return (hi,)
    return (hi, (x - hi.astype(F32)).astype(BF16))


def _mmx(a, b, dims, passes, expand=None):
    ap = _split_bf16(a, passes)
    bp = _split_bf16(b, passes)
    if expand is not None:
        bp = tuple(expand(part) for part in bp)
    out = _dot(ap[0], bp[0], dims)
    if passes == 3:
        out = (_dot(ap[1], bp[0], dims) + _dot(ap[0], bp[1], dims)) + out
    return out


_NN = ((1,), (0,))
_NT = ((1,), (1,))
_TN = ((0,), (0,))
WKV_PASSES_SCORES = 1
WKV_PASSES_INVERSE = 1
WKV_PASSES_STATE = 1
WKV_PASSES_VALUES = 1
WKV_INV_BASE = 16


def _wkv_kernel(rf_ref, vf_ref, kkf_ref, lwf_ref, kef_ref, bef_ref,
                rb_ref, vb_ref, kkb_ref, lwb_ref, keb_ref, beb_ref,
                yf_ref, yb_ref, ht_ref):
    c = WKV_CHUNK
    nchunks = rf_ref.shape[1] // c
    nh = A_HEADS

    @pl.when(pl.program_id(1) == 0)
    def _():
        ht_ref[...] = jnp.zeros_like(ht_ref)

    row = lax.broadcasted_iota(jnp.int32, (c, c), 0)
    col = lax.broadcasted_iota(jnp.int32, (c, c), 1)
    row_w = lax.broadcasted_iota(jnp.int32, (c, A_W), 0)
    col_w = lax.broadcasted_iota(jnp.int32, (c, A_W), 1) % c
    blk_r = lax.broadcasted_iota(jnp.int32, (A_W, A_W), 0) // HEAD_DIM
    blk_c = lax.broadcasted_iota(jnp.int32, (A_W, A_W), 1) // HEAD_DIM
    same_head = blk_r == blk_c
    eye_w = (row_w == col_w).astype(F32)

    def blockdiag(x):
        return jnp.where(same_head, jnp.concatenate([x] * nh, axis=0), jnp.zeros((), x.dtype))

    def unit_triangular_inverse(a_kb):
        base = WKV_INV_BASE
        same_base = (row_w // base) == (col_w // base)
        p = jnp.where(same_base, -a_kb, 0.0)
        tinv = eye_w + p
        p = _mmx(p, p, _NN, WKV_PASSES_INVERSE, blockdiag)
        yield
        span = 2
        while span < base:
            last = 2 * span >= base
            both = _mmx(tinv if last else jnp.concatenate([tinv, p], axis=0), p, _NN, WKV_PASSES_INVERSE, blockdiag)
            yield
            tinv = tinv + both[:c]
            if not last:
                p = both[c:]
            span *= 2
        size = base
        while size < c:
            coupling = ((row_w // (2 * size)) == (col_w // (2 * size))) & ((row_w // size) != (col_w // size))
            w = _mmx(jnp.where(coupling, a_kb, 0.0), tinv, _NN, WKV_PASSES_INVERSE, blockdiag)
            yield
            tinv = tinv - _mmx(tinv, w, _NN, WKV_PASSES_INVERSE, blockdiag)
            yield
            size *= 2
        return tinv

    def one_chunk(rev, bb, start, r_ref, v_ref, kk_ref, lw_ref, ke_ref, be_ref, y_ref, ht_box, slot):
        rows = pl.ds(start, c)
        incl = (col >= row) if rev else (col <= row)
        incl_w = (col_w >= row_w) if rev else (col_w <= row_w)
        strict_w = (col_w > row_w) if rev else (col_w < row_w)

        lw = lw_ref[0, bb, rows, :]
        cum = _mm_split_rhs(incl.astype(BF16), lw)
        g_tot = jnp.exp(jnp.sum(lw, axis=0, keepdims=True))
        g_inv = jnp.exp(-cum)
        v = v_ref[bb, rows, :]
        qt = r_ref[bb, rows, :] * jnp.exp(cum)
        kt = kk_ref[bb, rows, :] * jnp.exp(cum - lw)
        kb = ke_ref[0, bb, rows, :] * g_inv
        bt = be_ref[0, bb, rows, :] * g_inv
        kq = jnp.concatenate([kt, qt], axis=0)

        a_k = _mmx(kq, kb, _NT, WKV_PASSES_SCORES, blockdiag)
        yield
        a_b = _mmx(kq, bt, _NT, WKV_PASSES_SCORES, blockdiag)
        yield
        a_kk = jnp.where(strict_w, a_k[:c], 0.0)
        a_rk = jnp.where(incl_w, a_k[c:], 0.0)
        a_kb = jnp.where(strict_w, a_b[:c], 0.0)
        a_rb = jnp.where(incl_w, a_b[c:], 0.0)
        av = _mmx(jnp.concatenate([a_kk, a_rk], axis=0), v, _NN, WKV_PASSES_VALUES, blockdiag)
        yield
        tinv = yield from unit_triangular_inverse(a_kb)

        while ht_box[slot] is None:
            yield
        ht = ht_box[slot]
        hq = _mmx(kq, ht, _NT, WKV_PASSES_STATE)
        yield
        u = _mmx(tinv, hq[:c] + av[:c], _NN, WKV_PASSES_VALUES, blockdiag)
        yield
        upd = _mmx(jnp.concatenate([v, u], axis=0),
                   jnp.concatenate([kb * g_tot, -(bt * g_tot)], axis=0), _TN, WKV_PASSES_STATE)
        ht_box[slot + 1] = ht * g_tot + jnp.where(same_head, upd, 0.0)
        yield
        y_ref[bb, rows, :] = hq[c:] + av[c:] - _mmx(a_rb, u, _NN, WKV_PASSES_VALUES, blockdiag)

    nseq = rf_ref.shape[0]
    boxes = [[[ht_ref[e, bb]] + [None] * nchunks for bb in range(nseq)] for e in range(2)]
    tasks = []
    for ci in range(nchunks):
        for bb in range(nseq):
            tasks.append(one_chunk(False, bb, ci * c, rf_ref, vf_ref, kkf_ref, lwf_ref, kef_ref, bef_ref, yf_ref,
                                   boxes[0][bb], ci))
            tasks.append(one_chunk(True, bb, (nchunks - 1 - ci) * c, rb_ref, vb_ref, kkb_ref, lwb_ref, keb_ref,
                                   beb_ref, yb_ref, boxes[1][bb], ci))
    while tasks:
        alive = []
        for task in tasks:
            try:
                next(task)
                alive.append(task)
            except StopIteration:
                pass
        tasks = alive
    for e in range(2):
        for bb in range(nseq):
            ht_ref[e, bb] = boxes[e][bb][nchunks]


def _wkv_call(r, v, kk, logw, keff, beta):
    b, t, _ = r.shape
    tb = WKV_ROWS
    nseq = min(WKV_SEQS, b)
    nblk = t // tb
    grid = (b // nseq, nblk)
    tok_f = pl.BlockSpec((nseq, tb, A_W), lambda bi, j: (bi, j, 0))
    tok_b = pl.BlockSpec((nseq, tb, A_W), lambda bi, j: (bi, nblk - 1 - j, 0))
    dir_f = pl.BlockSpec((1, nseq, tb, A_W), lambda bi, j: (0, bi, j, 0))
    dir_b = pl.BlockSpec((1, nseq, tb, A_W), lambda bi, j: (1, bi, nblk - 1 - j, 0))
    out = jax.ShapeDtypeStruct((b, t, A_W), F32)
    return pl.pallas_call(
        _wkv_kernel, grid=grid,
        in_specs=[tok_f, tok_f, tok_f, dir_f, dir_f, dir_f, tok_b, tok_b, tok_b, dir_b, dir_b, dir_b],
        out_specs=[tok_f, tok_b], out_shape=[out, out],
        scratch_shapes=[pltpu.VMEM((2, nseq, A_W, A_W), F32)],
        compiler_params=pltpu.CompilerParams(
            dimension_semantics=("arbitrary", "arbitrary"), vmem_limit_bytes=VMEM_LIMIT),
        name="wkv",
    )(r, v, kk, logw, keff, beta, r, v, kk, logw, keff, beta)


def _band_kernel(*refs, radius, hq, hkv, seq_len, has_sink, want_lse, qblk, group):
    refs = list(refs)
    sink_ref = refs.pop(0) if has_sink else None
    q_ref, kp_ref, km_ref, kn_ref, vp_ref, vm_ref, vn_ref = refs[:7]
    o_ref = refs[7]
    lse_ref = refs[8] if want_lse else None
    kt_ref, vt_ref, ot_ref, m_ref, d_ref = refs[-5:]
    rad = radius
    tq = q_ref.shape[1]
    grp = hq // hkv
    win = qblk + 2 * rad
    nq = tq // qblk
    i = pl.program_id(1)

    def transposed(prev, main, nxt):
        cat = jnp.concatenate([prev[0], main[0], nxt[0]], axis=0)
        return cat.astype(F32).T.astype(BF16)

    kt_ref[...] = transposed(kp_ref, km_ref, kn_ref)
    vt_ref[...] = transposed(vp_ref, vm_ref, vn_ref)

    a = lax.broadcasted_iota(jnp.int32, (qblk, win), 0)
    cc = lax.broadcasted_iota(jnp.int32, (qblk, win), 1)
    band_bias = jnp.where(jnp.abs(cc - rad - a) <= rad, 0.0, NEG_INF)

    def bias_for(j):
        bias = band_bias
        kpos = i * tq + j * qblk - rad + cc
        if j == 0:
            bias = bias + jnp.where(kpos >= 0, 0.0, NEG_INF)
        if j == nq - 1:
            bias = bias + jnp.where(kpos < seq_len, 0.0, NEG_INF)
        return bias

    def scores(j, g, hh, bias):
        h = g * grp + hh
        q = q_ref[0, j * qblk:(j + 1) * qblk, h * HEAD_DIM:(h + 1) * HEAD_DIM]
        kwin = kt_ref[g * HEAD_DIM:(g + 1) * HEAD_DIM, j * qblk:j * qblk + win]
        s = _mm(q, kwin) + bias
        m = jnp.max(s, axis=-1, keepdims=True)
        if has_sink:
            m = jnp.maximum(m, sink_ref[h])
        p = jnp.exp(s - m)
        denom = jnp.sum(p, axis=-1, keepdims=True)
        if has_sink:
            denom = denom + jnp.exp(sink_ref[h] - m)
        return p.astype(BF16), m, denom

    def values(j, g, hh, p, m, denom):
        h = g * grp + hh
        qrows = slice(j * qblk, (j + 1) * qblk)
        hcols = slice(h * HEAD_DIM, (h + 1) * HEAD_DIM)
        vwin = vt_ref[g * HEAD_DIM:(g + 1) * HEAD_DIM, j * qblk:j * qblk + win]
        ot_ref[hcols, qrows] = _mm_nt(vwin, p)
        d_ref[qrows, hcols] = jnp.broadcast_to(denom, (qblk, HEAD_DIM))
        if want_lse:
            m_ref[qrows, hcols] = jnp.broadcast_to(m, (qblk, HEAD_DIM))

    bodies = [(j, g, hh) for j in range(nq) for g in range(hkv) for hh in range(grp)]
    groups = [bodies[n:n + group] for n in range(0, len(bodies), group)]
    biases = {}

    def run_scores(grp_bodies):
        out = []
        for (j, g, hh) in grp_bodies:
            if j not in biases:
                biases[j] = bias_for(j)
            out.append(scores(j, g, hh, biases[j]))
        return out

    pending = run_scores(groups[0])
    for n, grp_bodies in enumerate(groups):
        nxt = run_scores(groups[n + 1]) if n + 1 < len(groups) else None
        for body, state in zip(grp_bodies, pending):
            values(*body, *state)
        pending = nxt

    denom = d_ref[...]
    o_ref[0] = (ot_ref[...].T / denom).astype(o_ref.dtype)
    if want_lse:
        lse_ref[0] = m_ref[...] + jnp.log(denom)


def _band_call(q, k, v, *, radius, hq, hkv, sink=None, want_lse=False, name):
    s, l, wq = q.shape
    wk = k.shape[2]
    tq = min(ROW_TILE, l)
    nb = tq // radius
    last = l // radius - 1
    grid = (s, l // tq)
    main_q = pl.BlockSpec((1, tq, wq), lambda si, i: (si, i, 0))
    main_k = pl.BlockSpec((1, tq, wk), lambda si, i: (si, i, 0))
    prev_k = pl.BlockSpec((1, radius, wk), lambda si, i: (si, jnp.maximum(i * nb - 1, 0), 0))
    next_k = pl.BlockSpec((1, radius, wk), lambda si, i: (si, jnp.minimum((i + 1) * nb, last), 0))
    in_specs = [main_q, prev_k, main_k, next_k, prev_k, main_k, next_k]
    args = [q, k, k, k, v, v, v]
    if sink is not None:
        in_specs = [pl.BlockSpec(memory_space=pltpu.SMEM)] + in_specs
        args = [sink] + args
    out_specs = [main_q]
    out_shape = [jax.ShapeDtypeStruct((s, l, wq), BF16)]
    if want_lse:
        out_specs.append(main_q)
        out_shape.append(jax.ShapeDtypeStruct((s, l, wq), F32))
    group = max(1, BAND_GROUP_ROWS // BAND_QBLK)
    kern = functools.partial(_band_kernel, radius=radius, hq=hq, hkv=hkv, seq_len=l,
                             has_sink=sink is not None, want_lse=want_lse, qblk=BAND_QBLK, group=group)
    return pl.pallas_call(
        kern, grid=grid, in_specs=in_specs, out_specs=out_specs, out_shape=out_shape,
        scratch_shapes=[pltpu.VMEM((wk, tq + 2 * radius), BF16), pltpu.VMEM((wk, tq + 2 * radius), BF16),
                        pltpu.VMEM((wq, tq), F32), pltpu.VMEM((tq, wq), F32), pltpu.VMEM((tq, wq), F32)],
        compiler_params=pltpu.CompilerParams(
            dimension_semantics=("arbitrary", "arbitrary"), vmem_limit_bytes=VMEM_LIMIT),
        name=name,
    )(*args)


def _post_kernel(*refs, final):
    refs = list(refs)
    (x_ref, yf_ref, yb_ref, bonus_ref, ag_ref, o1_ref, l1_ref, o4_ref, l4_ref, o16_ref, l16_ref,
     bg_ref, co_ref, cg_ref, lnw_ref, lnb_ref, gavg_ref, wo_ref) = refs[:18]
    fg_ref = refs[18] if final else None
    out_ref = refs[-5]
    so4, sl4, so16, sl16 = refs[-4:]
    tm = x_ref.shape[1]

    gavg = gavg_ref[...]
    ya = yf_ref[0] + yb_ref[0]
    mu = _mm_split_lhs(ya, gavg)
    dev = ya - mu
    var = _mm_split_lhs(dev * dev, gavg)
    yn = dev * lax.rsqrt(var + HEADNORM_EPS) * lnw_ref[...] + lnb_ref[...]
    mix_a = (yn + bonus_ref[0]) * ag_ref[0].astype(F32)

    nl = B_W // LANES

    def interleave(src_ref, dst, d):
        for r in range(d):
            for j in range(nl):
                dst[j, pl.ds(r, tm // d, stride=d), :] = src_ref[0, r, :, j * LANES:(j + 1) * LANES].astype(F32)
        return jnp.concatenate([dst[j] for j in range(nl)], axis=1)

    o4 = interleave(o4_ref, so4, 4)
    l4 = interleave(l4_ref, sl4, 4)
    o16 = interleave(o16_ref, so16, 16)
    l16 = interleave(l16_ref, sl16, 16)
    l1 = l1_ref[0]
    lmax = jnp.maximum(jnp.maximum(l1, l4), l16)
    w1 = jnp.exp(l1 - lmax)
    w4 = jnp.exp(l4 - lmax)
    w16 = jnp.exp(l16 - lmax)
    mix_b = (w1 * o1_ref[0].astype(F32) + w4 * o4 + w16 * o16) / (w1 + w4 + w16) * bg_ref[0].astype(F32)

    mix_c = co_ref[0].astype(F32) * cg_ref[0].astype(F32)
    xn = (x_ref[0]
          + _mm(mix_a.astype(BF16), wo_ref[0:A_W, :])
          + _mm(mix_b.astype(BF16), wo_ref[A_W:A_W + B_W, :])
          + _mm(mix_c.astype(BF16), wo_ref[A_W + B_W:, :]))
    if final:
        xn = xn * lax.rsqrt(jnp.mean(xn * xn, axis=-1, keepdims=True) + RMS_EPS) * fg_ref[...]
    out_ref[0] = xn


def _post_call(x, yf, yb, bonus, ag, o1, l1, o4, l4, o16, l16, bg, co, cg, lnw, lnb, gavg, wo_bf16, final_g):
    b, t, _ = x.shape
    tm = ROW_TILE
    grid = (b, t // tm)
    final = final_g is not None

    def row(w):
        return pl.BlockSpec((1, tm, w), lambda bi, i: (bi, i, 0))

    def strided(d):
        return pl.BlockSpec((1, d, tm // d, B_W), lambda bi, i: (bi, 0, i, 0))

    def const(shape):
        return pl.BlockSpec(shape, lambda bi, i: (0,) * len(shape))

    in_specs = [row(D_MODEL), row(A_W), row(A_W),
                row(A_W), row(A_W), row(B_W), row(B_W), strided(4), strided(4), strided(16), strided(16),
                row(B_W), row(C_QW), row(C_QW),
                const((1, A_W)), const((1, A_W)), const((A_W, A_W)), const((D_MODEL, D_MODEL))]
    args = [x, yf, yb, bonus, ag, o1, l1, o4, l4, o16, l16, bg, co, cg, lnw, lnb, gavg, wo_bf16]
    if final:
        in_specs.append(const((1, D_MODEL)))
        args.append(final_g)
    return pl.pallas_call(
        functools.partial(_post_kernel, final=final), grid=grid, in_specs=in_specs,
        out_specs=row(D_MODEL), out_shape=jax.ShapeDtypeStruct((b, t, D_MODEL), F32),
        scratch_shapes=[pltpu.VMEM((B_W // LANES, tm, LANES), F32)] * 4,
        compiler_params=pltpu.CompilerParams(
            dimension_semantics=("arbitrary", "arbitrary"), vmem_limit_bytes=VMEM_LIMIT),
        name="post",
    )(*args)


def _rope_tables(t):
    inv = ROPE_THETA ** (-jnp.arange(0, HEAD_DIM, 2, dtype=F32) / HEAD_DIM)
    ang = jnp.arange(t, dtype=F32)[:, None] * inv[None, :]
    cos = jnp.cos(ang)
    sin = jnp.sin(ang)
    reps = LANES // HEAD_DIM
    cos_t = jnp.tile(jnp.concatenate([cos, cos], axis=1), (1, reps))
    sin_t = jnp.tile(jnp.concatenate([-sin, sin], axis=1), (1, reps))
    return cos_t, sin_t


def _head_block_matrix(value):
    idx = np.arange(A_W) // HEAD_DIM
    return jnp.asarray((idx[:, None] == idx[None, :]).astype(np.float32) * value, dtype=BF16)


def _lora_weights(w2):
    z = jnp.zeros((LORA, A_W), F32)
    top = jnp.concatenate([jnp.concatenate([w2[0], z], axis=1), jnp.concatenate([z, w2[1]], axis=1)], axis=0)
    pad = jnp.zeros((2 * LORA, 2 * A_W), F32)
    return top, pad


def _layer(x, p, tables, gsum, gavg, final_g):
    b, t, _ = x.shape
    cos_t, sin_t = tables
    (shift, ag, bg, cq, ck, cv, cg,
     bq1, bk1, bv1, bq4, bk4, bv4, bq16, bk16, bv16) = _proj_call(x, p["norm_g"], p["w_in"], cos_t, sin_t)

    r, v, kk, bonus, logw, keff, beta = _prep_call(
        shift, p["mu"], p["w0"], p["w2p"], p["a0"], p["a2p"], p["k_k"], p["k_a"], p["r_k"], gsum)
    yf, yb = _wkv_call(r, v, kk, logw, keff, beta)

    branch = []
    for (window, d), (q, k, vv) in zip(DILATED_PAIRS, ((bq1, bk1, bv1), (bq4, bk4, bv4), (bq16, bk16, bv16))):
        l = t // d
        o, lse = _band_call(q.reshape(b * d, l, B_W), k.reshape(b * d, l, B_W), vv.reshape(b * d, l, B_W),
                            radius=window // (2 * d), hq=B_HEADS, hkv=B_HEADS, want_lse=True, name=f"dil{d}")
        branch += [o.reshape(b, d, l, B_W) if d > 1 else o, lse.reshape(b, d, l, B_W) if d > 1 else lse]
    (co,) = _band_call(cq, ck, cv, radius=C_RADIUS, hq=C_HEADS, hkv=C_KV_HEADS, sink=p["sink"], name="win")

    return _post_call(x, yf, yb, bonus, ag, *branch, bg, co, cg, p["ln_w"], p["ln_b"], gavg, p["w_out"], final_g)


def _trunk(x, layers, final_g, gsum, gavg):
    tables = _rope_tables(x.shape[1])
    for li, p in enumerate(layers):
        x = _layer(x, p, tables, gsum, gavg, final_g if li == len(layers) - 1 else None)
    return x


def kernel(x_prompt, x_sample, norm_g, w_in, tshift_mu, rwkv_w0, rwkv_w2, rwkv_a0, rwkv_a2, rwkv_k_k, rwkv_k_a,
           rwkv_r_k, ln_x_w, ln_x_b, attn_sink, w_out, final_g):
    depth = norm_g.shape[0]
    layers = []
    for l in range(depth):
        w2_top, pad = _lora_weights(rwkv_w2[l])
        a2_top, _ = _lora_weights(rwkv_a2[l])
        layers.append(dict(
            norm_g=norm_g[l][None, :], w_in=w_in[l].astype(BF16), mu=tshift_mu[l][None, :],
            w0=rwkv_w0[l], w2p=jnp.concatenate([w2_top, pad], axis=0),
            a0=rwkv_a0[l], a2p=jnp.concatenate([pad, a2_top], axis=0),
            k_k=rwkv_k_k[l][None, :], k_a=rwkv_k_a[l][None, :], r_k=rwkv_r_k[l].reshape(1, A_W),
            ln_w=ln_x_w[l][None, :], ln_b=ln_x_b[l][None, :], sink=attn_sink[l],
            w_out=w_out[l].astype(BF16)))
    gsum = _head_block_matrix(1.0)
    gavg = _head_block_matrix(1.0 / HEAD_DIM)
    fg = final_g[None, :]
    return (_trunk(x_prompt, layers, fg, gsum, gavg), _trunk(x_sample, layers, fg, gsum, gavg))
```

```python
import functools
import math

import jax
import jax.numpy as jnp
import numpy as np
from jax import lax
from jax.experimental import pallas as pl
from jax.experimental.pallas import tpu as pltpu

F32 = jnp.float32
BF16 = jnp.bfloat16

D_MODEL = 1024
HEAD_DIM = 64
A_HEADS = 4
A_W = A_HEADS * HEAD_DIM
LORA = 64
B_HEADS = 4
B_W = B_HEADS * HEAD_DIM
DILATED_PAIRS = ((128, 1), (512, 4), (2048, 16))
C_HEADS = 8
C_KV_HEADS = 2
C_QW = C_HEADS * HEAD_DIM
C_KVW = C_KV_HEADS * HEAD_DIM
C_RADIUS = 128
ROPE_THETA = 10000.0
RMS_EPS = 1e-5
HEADNORM_EPS = 64e-5
NEG_INF = -1e30
TSHIFT_W = 3 * A_W + 4 * LORA
IN_W = 3584

TOK_R, TOK_V, TOK_KK, TOK_BONUS, TOK_W = 0, A_W, 2 * A_W, 3 * A_W, 4 * A_W
TOK_SCAN_W = 3 * A_W
DIR_LOGW, DIR_KEFF, DIR_BETA, DIR_W = 0, A_W, 2 * A_W, 3 * A_W
GATE_A, GATE_B, GATE_C, GATE_W = 0, A_W, A_W + B_W, A_W + B_W + C_QW
BQKV_W = 3 * B_W
CQKV_W = C_QW + 2 * C_KVW

COL_AG = TSHIFT_W
COL_BQ = COL_AG + A_W
COL_BK = COL_BQ + B_W
COL_BV = COL_BK + B_W
COL_BG = COL_BV + B_W
COL_CQ = COL_BG + B_W
COL_CK = COL_CQ + C_QW
COL_CV = COL_CK + C_KVW
COL_CG = COL_CV + C_KVW

LANES = 128
ROW_TILE = 512
WKV_CHUNK = 64
WKV_ROWS = 256
WKV_SEQS = 2
HALO_ROWS = 8
BAND_GROUP_DIL = 2
BAND_GROUP_WIN = 8
BAND_QBLK = 128
VMEM_LIMIT = 56 * 1024 * 1024


def _dot(a, b, dims):
    return lax.dot_general(a, b, (dims, ((), ())), preferred_element_type=F32)


def _mm(a, b):
    return _dot(a, b, ((1,), (0,)))


def _mm_nt(a, b):
    return _dot(a, b, ((1,), (1,)))


def _bf16_terms(x, n):
    terms = []
    for _ in range(n):
        t = x.astype(BF16)
        terms.append(t)
        x = x - t.astype(F32)
    return terms


def _mm_split_rhs(a_bf16, b, n=3):
    parts = _bf16_terms(b, n)
    out = _mm(a_bf16, parts[-1])
    for t in reversed(parts[:-1]):
        out = out + _mm(a_bf16, t)
    return out


def _sigmoid(x):
    return 1.0 / (1.0 + jnp.exp(-x))


def _silu(x):
    return x * _sigmoid(x)


def _softplus(x):
    return jnp.maximum(x, 0.0) + jnp.log(1.0 + jnp.exp(-jnp.abs(x)))


def _proj_kernel(x_ref, xp_ref, xn_ref, g_ref, w_ref, cos_ref, sin_ref,
                 mu_ref, w0_ref, w2_ref, a0_ref, a2_ref, kk_ref_, ka_ref, rk_ref, gsum_ref,
                 tok_ref, dir_ref, gate_ref, c_ref, b1_ref, b4_ref, b16_ref,
                 tmp_ref):
    i = pl.program_id(1)
    n = pl.num_programs(1)
    tm = x_ref.shape[1]
    g = g_ref[...]

    def normed(x):
        return (x * lax.rsqrt(jnp.mean(x * x, axis=-1, keepdims=True) + RMS_EPS) * g).astype(BF16)

    hb_all = jnp.concatenate([normed(x_ref[0]), normed(xp_ref[0]), normed(xn_ref[0])], axis=0)
    hb = hb_all[:tm]
    prev_at = tm + HALO_ROWS - 1
    next_at = tm + HALO_ROWS
    row = lax.broadcasted_iota(jnp.int32, (tm, 1), 0)

    def shifted(c0, c1):
        full = _mm(hb_all, w_ref[:, c0:c1])
        f = full[:tm]
        prev_row = jnp.where(i > 0, full[prev_at:prev_at + 1], 0.0)
        next_row = jnp.where(i < n - 1, full[next_at:next_at + 1], 0.0)
        prev = jnp.where(row == 0, prev_row, pltpu.roll(f, 1, axis=0))
        nxt = jnp.where(row == tm - 1, next_row, pltpu.roll(f, tm - 1, axis=0))
        return f + mu_ref[:, c0:c1] * (0.5 * (prev + nxt) - f)

    def proj(c0, c1):
        return _mm(hb, w_ref[:, c0:c1])

    r = shifted(0, A_W)
    k = shifted(A_W, 2 * A_W)
    v = shifted(2 * A_W, 3 * A_W)
    lora_in = shifted(3 * A_W, TSHIFT_W)
    wl = _mm(jnp.tanh(lora_in).astype(BF16), w2_ref[...])
    al = _mm(lora_in.astype(BF16), a2_ref[...])
    gsum = gsum_ref[...]
    kk0 = k * kk_ref_[...]
    ss = _mm((kk0 * kk0).astype(BF16), gsum)
    kkn = kk0 / jnp.maximum(jnp.sqrt(ss), 1e-12)
    tok_ref[0, :, TOK_R:TOK_R + A_W] = r
    tok_ref[0, :, TOK_V:TOK_V + A_W] = v
    tok_ref[0, :, TOK_KK:TOK_KK + A_W] = kkn
    keff_sum = jnp.zeros_like(k)
    for e in range(2):
        w = -_softplus(-(w0_ref[e:e + 1, :] + wl[:, e * A_W:(e + 1) * A_W])) - 0.5
        dir_ref[e, 0, :, DIR_LOGW:DIR_LOGW + A_W] = -jnp.exp(w)
        a = _sigmoid(a0_ref[e:e + 1, :] + al[:, e * A_W:(e + 1) * A_W])
        keff = k * (1.0 + (a - 1.0) * ka_ref[...])
        dir_ref[e, 0, :, DIR_KEFF:DIR_KEFF + A_W] = keff
        dir_ref[e, 0, :, DIR_BETA:DIR_BETA + A_W] = a * kkn
        keff_sum = keff_sum + keff
    tok_ref[0, :, TOK_BONUS:TOK_BONUS + A_W] = _mm((r * keff_sum * rk_ref[...]).astype(BF16), gsum) * v

    cos = cos_ref[...]
    sin = sin_ref[...]
    lane = lax.broadcasted_iota(jnp.int32, (tm, LANES), 1)
    first_half = (lane % HEAD_DIM) < (HEAD_DIM // 2)

    def rope(t):
        outs = []
        for j in range(t.shape[1] // LANES):
            tj = t[:, j * LANES:(j + 1) * LANES]
            partner = jnp.where(first_half,
                                pltpu.roll(tj, LANES - HEAD_DIM // 2, axis=1),
                                pltpu.roll(tj, HEAD_DIM // 2, axis=1))
            outs.append(tj * cos + partner * sin)
        return jnp.concatenate(outs, axis=1) if len(outs) > 1 else outs[0]

    def deinterleave(val, col):
        b1_ref[0, :, col:col + B_W] = val.astype(BF16)
        nl = B_W // LANES
        for j in range(nl):
            tmp_ref[j] = val[:, j * LANES:(j + 1) * LANES]
        for d, ref in ((4, b4_ref), (16, b16_ref)):
            for rr in range(d):
                parts = [tmp_ref[j, pl.ds(rr, tm // d, stride=d), :] for j in range(nl)]
                ref[0, rr, :, col:col + B_W] = jnp.concatenate(parts, axis=1).astype(BF16)

    gate_ref[0, :, GATE_A:GATE_A + A_W] = _silu(proj(COL_AG, COL_AG + A_W)).astype(BF16)
    gate_ref[0, :, GATE_B:GATE_B + B_W] = _silu(proj(COL_BG, COL_BG + B_W)).astype(BF16)
    gate_ref[0, :, GATE_C:GATE_C + C_QW] = _silu(proj(COL_CG, COL_CG + C_QW)).astype(BF16)
    scale = HEAD_DIM ** -0.5
    deinterleave(rope(proj(COL_BQ, COL_BQ + B_W)) * scale, 0)
    deinterleave(rope(proj(COL_BK, COL_BK + B_W)), B_W)
    deinterleave(proj(COL_BV, COL_BV + B_W), 2 * B_W)
    for c0 in range(0, C_QW, 256):
        c_ref[0, :, c0:c0 + 256] = (rope(proj(COL_CQ + c0, COL_CQ + c0 + 256)) * scale).astype(BF16)
    c_ref[0, :, C_QW:C_QW + C_KVW] = rope(proj(COL_CK, COL_CK + C_KVW)).astype(BF16)
    c_ref[0, :, C_QW + C_KVW:C_QW + 2 * C_KVW] = proj(COL_CV, COL_CV + C_KVW).astype(BF16)


def _proj_call(x, g, w_bf16, cos_t, sin_t, mu, w0, w2p, a0, a2p, k_k, k_a, r_k, gsum):
    b, t, _ = x.shape
    tm = ROW_TILE
    nb = tm // HALO_ROWS
    last = t // HALO_ROWS - 1
    grid = (b, t // tm)

    def row(w):
        return pl.BlockSpec((1, tm, w), lambda bi, i: (bi, i, 0))

    def strided(d):
        return pl.BlockSpec((1, d, tm // d, BQKV_W), lambda bi, i: (bi, 0, i, 0))

    def const(shape):
        return pl.BlockSpec(shape, lambda bi, i: (0,) * len(shape))

    def nat(w, dt):
        return jax.ShapeDtypeStruct((b, t, w), dt)

    def sshape(d):
        return jax.ShapeDtypeStruct((b, d, t // d, BQKV_W), BF16)

    out_shape = [nat(TOK_W, F32), jax.ShapeDtypeStruct((2, b, t, DIR_W), F32), nat(GATE_W, BF16),
                 nat(CQKV_W, BF16), nat(BQKV_W, BF16), sshape(4), sshape(16)]
    out_specs = [row(TOK_W), pl.BlockSpec((2, 1, tm, DIR_W), lambda bi, i: (0, bi, i, 0)), row(GATE_W),
                 row(CQKV_W), row(BQKV_W), strided(4), strided(16)]
    in_specs = [row(D_MODEL),
                pl.BlockSpec((1, HALO_ROWS, D_MODEL), lambda bi, i: (bi, jnp.maximum(i * nb - 1, 0), 0)),
                pl.BlockSpec((1, HALO_ROWS, D_MODEL), lambda bi, i: (bi, jnp.minimum((i + 1) * nb, last), 0)),
                const((1, D_MODEL)), const((D_MODEL, IN_W)),
                pl.BlockSpec((tm, LANES), lambda bi, i: (i, 0)),
                pl.BlockSpec((tm, LANES), lambda bi, i: (i, 0)),
                const((1, TSHIFT_W)), const((2, A_W)), const((4 * LORA, 2 * A_W)),
                const((2, A_W)), const((4 * LORA, 2 * A_W)),
                const((1, A_W)), const((1, A_W)), const((1, A_W)), const((A_W, A_W))]
    return pl.pallas_call(
        _proj_kernel, grid=grid, in_specs=in_specs, out_specs=out_specs, out_shape=out_shape,
        scratch_shapes=[pltpu.VMEM((B_W // LANES, tm, LANES), F32)],
        compiler_params=pltpu.CompilerParams(
            dimension_semantics=("arbitrary", "arbitrary"), vmem_limit_bytes=VMEM_LIMIT),
        name="proj",
    )(x, x, x, g, w_bf16, cos_t, sin_t, mu, w0, w2p, a0, a2p, k_k, k_a, r_k, gsum)


_NN = ((1,), (0,))
_NT = ((1,), (1,))
_TN = ((0,), (0,))
WKV_INV_BASE = 16


def _mmx(a, b, dims, expand=None):
    b = b.astype(BF16)
    return _dot(a.astype(BF16), b if expand is None else expand(b), dims)


def _wkv_kernel(tokf_ref, dirf_ref, tokb_ref, dirb_ref, yf_ref, yb_ref, ht_ref):
    c = WKV_CHUNK
    nchunks = tokf_ref.shape[1] // c
    nh = A_HEADS

    @pl.when(pl.program_id(1) == 0)
    def _():
        ht_ref[...] = jnp.zeros_like(ht_ref)

    row = lax.broadcasted_iota(jnp.int32, (c, c), 0)
    col = lax.broadcasted_iota(jnp.int32, (c, c), 1)
    row_w = lax.broadcasted_iota(jnp.int32, (c, A_W), 0)
    col_w = lax.broadcasted_iota(jnp.int32, (c, A_W), 1) % c
    blk_r = lax.broadcasted_iota(jnp.int32, (A_W, A_W), 0) // HEAD_DIM
    blk_c = lax.broadcasted_iota(jnp.int32, (A_W, A_W), 1) // HEAD_DIM
    same_head = blk_r == blk_c
    eye_w = (row_w == col_w).astype(F32)

    def blockdiag(x):
        return jnp.where(same_head, jnp.concatenate([x] * nh, axis=0), jnp.zeros((), x.dtype))

    def unit_triangular_inverse(a_kb):
        base = WKV_INV_BASE
        same_base = (row_w // base) == (col_w // base)
        p = jnp.where(same_base, -a_kb, 0.0)
        tinv = eye_w + p
        p = _mmx(p, p, _NN, blockdiag)
        yield
        span = 2
        while span < base:
            last = 2 * span >= base
            both = _mmx(tinv if last else jnp.concatenate([tinv, p], axis=0), p, _NN, blockdiag)
            yield
            tinv = tinv + both[:c]
            if not last:
                p = both[c:]
            span *= 2
        size = base
        while size < c:
            coupling = ((row_w // (2 * size)) == (col_w // (2 * size))) & ((row_w // size) != (col_w // size))
            w = _mmx(jnp.where(coupling, a_kb, 0.0), tinv, _NN, blockdiag)
            yield
            tinv = tinv - _mmx(tinv, w, _NN, blockdiag)
            yield
            size *= 2
        return tinv

    def one_chunk(rev, bb, start, tok_ref, dir_ref, y_ref, ht_box, slot):
        rows = pl.ds(start, c)
        incl = (col >= row) if rev else (col <= row)
        incl_w = (col_w >= row_w) if rev else (col_w <= row_w)
        strict_w = (col_w > row_w) if rev else (col_w < row_w)

        lw = dir_ref[0, bb, rows, DIR_LOGW:DIR_LOGW + A_W]
        cum = _mm_split_rhs(incl.astype(BF16), lw)
        g_tot = jnp.exp(jnp.sum(lw, axis=0, keepdims=True))
        g_inv = jnp.exp(-cum)
        v = tok_ref[bb, rows, TOK_V:TOK_V + A_W]
        qt = tok_ref[bb, rows, TOK_R:TOK_R + A_W] * jnp.exp(cum)
        kt = tok_ref[bb, rows, TOK_KK:TOK_KK + A_W] * jnp.exp(cum - lw)
        kb = dir_ref[0, bb, rows, DIR_KEFF:DIR_KEFF + A_W] * g_inv
        bt = dir_ref[0, bb, rows, DIR_BETA:DIR_BETA + A_W] * g_inv
        kq = jnp.concatenate([kt, qt], axis=0)

        a_k = _mmx(kq, kb, _NT, blockdiag)
        yield
        a_b = _mmx(kq, bt, _NT, blockdiag)
        yield
        a_kk = jnp.where(strict_w, a_k[:c], 0.0)
        a_rk = jnp.where(incl_w, a_k[c:], 0.0)
        a_kb = jnp.where(strict_w, a_b[:c], 0.0)
        a_rb = jnp.where(incl_w, a_b[c:], 0.0)
        av = _mmx(jnp.concatenate([a_kk, a_rk], axis=0), v, _NN, blockdiag)
        yield
        tinv = yield from unit_triangular_inverse(a_kb)

        while ht_box[slot] is None:
            yield
        ht = ht_box[slot]
        hq = _mmx(kq, ht, _NT)
        yield
        u = _mmx(tinv, hq[:c] + av[:c], _NN, blockdiag)
        yield
        upd = _mmx(jnp.concatenate([v, u], axis=0), jnp.concatenate([kb * g_tot, -(bt * g_tot)], axis=0), _TN)
        ht_box[slot + 1] = ht * g_tot + jnp.where(same_head, upd, 0.0)
        yield
        y_ref[bb, rows, :] = hq[c:] + av[c:] - _mmx(a_rb, u, _NN, blockdiag)

    nseq = tokf_ref.shape[0]
    boxes = [[[ht_ref[e, bb]] + [None] * nchunks for bb in range(nseq)] for e in range(2)]
    tasks = []
    for ci in range(nchunks):
        for bb in range(nseq):
            tasks.append(one_chunk(False, bb, ci * c, tokf_ref, dirf_ref, yf_ref, boxes[0][bb], ci))
            tasks.append(one_chunk(True, bb, (nchunks - 1 - ci) * c, tokb_ref, dirb_ref, yb_ref, boxes[1][bb], ci))
    while tasks:
        alive = []
        for task in tasks:
            try:
                next(task)
                alive.append(task)
            except StopIteration:
                pass
        tasks = alive
    for e in range(2):
        for bb in range(nseq):
            ht_ref[e, bb] = boxes[e][bb][nchunks]


def _wkv_call(tok, dirq):
    b, t, _ = tok.shape
    tb = WKV_ROWS
    nseq = min(WKV_SEQS, b)
    nblk = t // tb
    grid = (b // nseq, nblk)
    tok_f = pl.BlockSpec((nseq, tb, TOK_SCAN_W), lambda bi, j: (bi, j, 0))
    tok_b = pl.BlockSpec((nseq, tb, TOK_SCAN_W), lambda bi, j: (bi, nblk - 1 - j, 0))
    dir_f = pl.BlockSpec((1, nseq, tb, DIR_W), lambda bi, j: (0, bi, j, 0))
    dir_b = pl.BlockSpec((1, nseq, tb, DIR_W), lambda bi, j: (1, bi, nblk - 1 - j, 0))
    y_f = pl.BlockSpec((nseq, tb, A_W), lambda bi, j: (bi, j, 0))
    y_b = pl.BlockSpec((nseq, tb, A_W), lambda bi, j: (bi, nblk - 1 - j, 0))
    out = jax.ShapeDtypeStruct((b, t, A_W), F32)
    return pl.pallas_call(
        _wkv_kernel, grid=grid, in_specs=[tok_f, dir_f, tok_b, dir_b],
        out_specs=[y_f, y_b], out_shape=[out, out],
        scratch_shapes=[pltpu.VMEM((2, nseq, A_W, A_W), F32)],
        compiler_params=pltpu.CompilerParams(
            dimension_semantics=("arbitrary", "arbitrary"), vmem_limit_bytes=VMEM_LIMIT),
        name="wkv",
    )(tok, dirq, tok, dirq)


def _band_kernel(*refs, radius, hq, hkv, seq_len, has_sink, want_lse, qblk, group):
    refs = list(refs)
    sink_ref = refs.pop(0) if has_sink else None
    prev_ref, main_ref, next_ref = refs[:3]
    o_ref = refs[3]
    lse_ref = refs[4] if want_lse else None
    kt_ref, vt_ref, ot_ref, m_ref, d_ref = refs[-5:]
    rad = radius
    tq = main_ref.shape[1]
    grp = hq // hkv
    wq = hq * HEAD_DIM
    wk = hkv * HEAD_DIM
    win = qblk + 2 * rad
    nq = tq // qblk
    i = pl.program_id(1)

    def transposed(c0):
        cat = jnp.concatenate([prev_ref[0, :, c0:c0 + wk], main_ref[0, :, c0:c0 + wk], next_ref[0, :, c0:c0 + wk]],
                              axis=0)
        return cat.astype(F32).T.astype(BF16)

    kt_ref[...] = transposed(wq)
    vt_ref[...] = transposed(wq + wk)

    a = lax.broadcasted_iota(jnp.int32, (qblk, win), 0)
    cc = lax.broadcasted_iota(jnp.int32, (qblk, win), 1)
    band_bias = jnp.where(jnp.abs(cc - rad - a) <= rad, 0.0, NEG_INF)

    def bias_for(j):
        bias = band_bias
        kpos = i * tq + j * qblk - rad + cc
        if j == 0:
            bias = bias + jnp.where(kpos >= 0, 0.0, NEG_INF)
        if j == nq - 1:
            bias = bias + jnp.where(kpos < seq_len, 0.0, NEG_INF)
        return bias

    def scores(j, g, hh, bias):
        h = g * grp + hh
        q = main_ref[0, j * qblk:(j + 1) * qblk, h * HEAD_DIM:(h + 1) * HEAD_DIM]
        kwin = kt_ref[g * HEAD_DIM:(g + 1) * HEAD_DIM, j * qblk:j * qblk + win]
        s = _mm(q, kwin) + bias
        m = jnp.max(s, axis=-1, keepdims=True)
        if has_sink:
            m = jnp.maximum(m, sink_ref[h])
        p = jnp.exp(s - m)
        denom = jnp.sum(p, axis=-1, keepdims=True)
        if has_sink:
            denom = denom + jnp.exp(sink_ref[h] - m)
        return p.astype(BF16), m, denom

    def values(j, g, hh, p, m, denom):
        h = g * grp + hh
        qrows = slice(j * qblk, (j + 1) * qblk)
        hcols = slice(h * HEAD_DIM, (h + 1) * HEAD_DIM)
        vwin = vt_ref[g * HEAD_DIM:(g + 1) * HEAD_DIM, j * qblk:j * qblk + win]
        ot_ref[hcols, qrows] = _mm_nt(vwin, p)
        d_ref[qrows, hcols] = jnp.broadcast_to(denom, (qblk, HEAD_DIM))
        if want_lse:
            m_ref[qrows, hcols] = jnp.broadcast_to(m, (qblk, HEAD_DIM))

    bodies = [(j, g, hh) for j in range(nq) for g in range(hkv) for hh in range(grp)]
    groups = [bodies[n:n + group] for n in range(0, len(bodies), group)]
    biases = {}

    def run_scores(grp_bodies):
        out = []
        for (j, g, hh) in grp_bodies:
            if j not in biases:
                biases[j] = bias_for(j)
            out.append(scores(j, g, hh, biases[j]))
        return out

    pending = run_scores(groups[0])
    for n, grp_bodies in enumerate(groups):
        nxt = run_scores(groups[n + 1]) if n + 1 < len(groups) else None
        for body, state in zip(grp_bodies, pending):
            values(*body, *state)
        pending = nxt

    denom = d_ref[...]
    o_ref[0] = (ot_ref[...].T / denom).astype(o_ref.dtype)
    if want_lse:
        lse_ref[0] = m_ref[...] + jnp.log(denom)


def _band_call(qkv, *, radius, hq, hkv, group, sink=None, want_lse=False, name):
    s, l, w = qkv.shape
    wq = hq * HEAD_DIM
    wk = hkv * HEAD_DIM
    tq = min(ROW_TILE, l)
    nb = tq // radius
    last = l // radius - 1
    grid = (s, l // tq)
    main = pl.BlockSpec((1, tq, w), lambda si, i: (si, i, 0))
    prev = pl.BlockSpec((1, radius, w), lambda si, i: (si, jnp.maximum(i * nb - 1, 0), 0))
    nxt = pl.BlockSpec((1, radius, w), lambda si, i: (si, jnp.minimum((i + 1) * nb, last), 0))
    out_blk = pl.BlockSpec((1, tq, wq), lambda si, i: (si, i, 0))
    in_specs = [prev, main, nxt]
    args = [qkv, qkv, qkv]
    if sink is not None:
        in_specs = [pl.BlockSpec(memory_space=pltpu.SMEM)] + in_specs
        args = [sink] + args
    out_specs = [out_blk]
    out_shape = [jax.ShapeDtypeStruct((s, l, wq), BF16)]
    if want_lse:
        out_specs.append(out_blk)
        out_shape.append(jax.ShapeDtypeStruct((s, l, wq), F32))
    kern = functools.partial(_band_kernel, radius=radius, hq=hq, hkv=hkv, seq_len=l,
                             has_sink=sink is not None, want_lse=want_lse, qblk=BAND_QBLK, group=group)
    return pl.pallas_call(
        kern, grid=grid, in_specs=in_specs, out_specs=out_specs, out_shape=out_shape,
        scratch_shapes=[pltpu.VMEM((wk, tq + 2 * radius), BF16), pltpu.VMEM((wk, tq + 2 * radius), BF16),
                        pltpu.VMEM((wq, tq), F32), pltpu.VMEM((tq, wq), F32), pltpu.VMEM((tq, wq), F32)],
        compiler_params=pltpu.CompilerParams(
            dimension_semantics=("arbitrary", "arbitrary"), vmem_limit_bytes=VMEM_LIMIT),
        name=name,
    )(*args)


def _post_kernel(*refs, final):
    refs = list(refs)
    (x_ref, yf_ref, yb_ref, bonus_ref, gate_ref, o1_ref, l1_ref, o4_ref, l4_ref, o16_ref, l16_ref,
     co_ref, lnw_ref, lnb_ref, gavg_ref, wo_ref) = refs[:16]
    fg_ref = refs[16] if final else None
    out_ref = refs[-5]
    so4, sl4, so16, sl16 = refs[-4:]
    tm = x_ref.shape[1]

    gavg = gavg_ref[...]
    ya = yf_ref[0] + yb_ref[0]
    mu = _mm(ya.astype(BF16), gavg)
    dev = ya - mu
    var = _mm((dev * dev).astype(BF16), gavg)
    yn = dev * lax.rsqrt(var + HEADNORM_EPS) * lnw_ref[...] + lnb_ref[...]
    mix_a = (yn + bonus_ref[0]) * gate_ref[0, :, GATE_A:GATE_A + A_W].astype(F32)

    nl = B_W // LANES

    def interleave(src_ref, dst, d):
        for r in range(d):
            for j in range(nl):
                dst[j, pl.ds(r, tm // d, stride=d), :] = src_ref[0, r, :, j * LANES:(j + 1) * LANES].astype(F32)
        return jnp.concatenate([dst[j] for j in range(nl)], axis=1)

    o4 = interleave(o4_ref, so4, 4)
    l4 = interleave(l4_ref, sl4, 4)
    o16 = interleave(o16_ref, so16, 16)
    l16 = interleave(l16_ref, sl16, 16)
    l1 = l1_ref[0]
    lmax = jnp.maximum(jnp.maximum(l1, l4), l16)
    w1 = jnp.exp(l1 - lmax)
    w4 = jnp.exp(l4 - lmax)
    w16 = jnp.exp(l16 - lmax)
    mix_b = ((w1 * o1_ref[0].astype(F32) + w4 * o4 + w16 * o16) / (w1 + w4 + w16)
             * gate_ref[0, :, GATE_B:GATE_B + B_W].astype(F32))

    mix_c = co_ref[0].astype(F32) * gate_ref[0, :, GATE_C:GATE_C + C_QW].astype(F32)
    xn = (x_ref[0]
          + _mm(mix_a.astype(BF16), wo_ref[0:A_W, :])
          + _mm(mix_b.astype(BF16), wo_ref[A_W:A_W + B_W, :])
          + _mm(mix_c.astype(BF16), wo_ref[A_W + B_W:, :]))
    if final:
        xn = xn * lax.rsqrt(jnp.mean(xn * xn, axis=-1, keepdims=True) + RMS_EPS) * fg_ref[...]
    out_ref[0] = xn


def _post_call(x, yf, yb, tok, gates, o1, l1, o4, l4, o16, l16, co, lnw, lnb, gavg, wo_bf16, final_g):
    b, t, _ = x.shape
    tm = ROW_TILE
    grid = (b, t // tm)
    final = final_g is not None

    def row(w):
        return pl.BlockSpec((1, tm, w), lambda bi, i: (bi, i, 0))

    def strided(d):
        return pl.BlockSpec((1, d, tm // d, B_W), lambda bi, i: (bi, 0, i, 0))

    def const(shape):
        return pl.BlockSpec(shape, lambda bi, i: (0,) * len(shape))

    bonus_blk = pl.BlockSpec((1, tm, A_W), lambda bi, i: (bi, i, TOK_BONUS // A_W))
    in_specs = [row(D_MODEL), row(A_W), row(A_W), bonus_blk, row(GATE_W),
                row(B_W), row(B_W), strided(4), strided(4), strided(16), strided(16), row(C_QW),
                const((1, A_W)), const((1, A_W)), const((A_W, A_W)), const((D_MODEL, D_MODEL))]
    args = [x, yf, yb, tok, gates, o1, l1, o4, l4, o16, l16, co, lnw, lnb, gavg, wo_bf16]
    if final:
        in_specs.append(const((1, D_MODEL)))
        args.append(final_g)
    return pl.pallas_call(
        functools.partial(_post_kernel, final=final), grid=grid, in_specs=in_specs,
        out_specs=row(D_MODEL), out_shape=jax.ShapeDtypeStruct((b, t, D_MODEL), F32),
        scratch_shapes=[pltpu.VMEM((B_W // LANES, tm, LANES), F32)] * 4,
        compiler_params=pltpu.CompilerParams(
            dimension_semantics=("arbitrary", "arbitrary"), vmem_limit_bytes=VMEM_LIMIT),
        name="post",
    )(*args)


def _rope_tables(t):
    inv = ROPE_THETA ** (-jnp.arange(0, HEAD_DIM, 2, dtype=F32) / HEAD_DIM)
    ang = jnp.arange(t, dtype=F32)[:, None] * inv[None, :]
    cos = jnp.cos(ang)
    sin = jnp.sin(ang)
    reps = LANES // HEAD_DIM
    cos_t = jnp.tile(jnp.concatenate([cos, cos], axis=1), (1, reps))
    sin_t = jnp.tile(jnp.concatenate([-sin, sin], axis=1), (1, reps))
    return cos_t, sin_t


def _head_block_matrix(value):
    idx = np.arange(A_W) // HEAD_DIM
    return jnp.asarray((idx[:, None] == idx[None, :]).astype(np.float32) * value, dtype=BF16)


def _lora_weights(w2):
    z = jnp.zeros((LORA, A_W), F32)
    top = jnp.concatenate([jnp.concatenate([w2[0], z], axis=1), jnp.concatenate([z, w2[1]], axis=1)], axis=0)
    pad = jnp.zeros((2 * LORA, 2 * A_W), F32)
    return top, pad


def _layer(x, p, tables, gsum, gavg, final_g):
    b, t, _ = x.shape
    cos_t, sin_t = tables
    tok, dirq, gates, cqkv, b1, b4, b16 = _proj_call(
        x, p["norm_g"], p["w_in"], cos_t, sin_t,
        p["mu"], p["w0"], p["w2p"], p["a0"], p["a2p"], p["k_k"], p["k_a"], p["r_k"], gsum)
    yf, yb = _wkv_call(tok, dirq)

    branch = []
    for (window, d), qkv in zip(DILATED_PAIRS, (b1, b4, b16)):
        l = t // d
        o, lse = _band_call(qkv.reshape(b * d, l, BQKV_W), radius=window // (2 * d), hq=B_HEADS, hkv=B_HEADS,
                            group=BAND_GROUP_DIL, want_lse=True, name=f"dil{d}")
        branch += [o.reshape(b, d, l, B_W) if d > 1 else o, lse.reshape(b, d, l, B_W) if d > 1 else lse]
    (co,) = _band_call(cqkv, radius=C_RADIUS, hq=C_HEADS, hkv=C_KV_HEADS, group=BAND_GROUP_WIN,
                       sink=p["sink"], name="win")

    return _post_call(x, yf, yb, tok, gates, *branch, co, p["ln_w"], p["ln_b"], gavg, p["w_out"], final_g)


def _trunk(x, layers, final_g, gsum, gavg):
    tables = _rope_tables(x.shape[1])
    for li, p in enumerate(layers):
        x = _layer(x, p, tables, gsum, gavg, final_g if li == len(layers) - 1 else None)
    return x


def kernel(x_prompt, x_sample, norm_g, w_in, tshift_mu, rwkv_w0, rwkv_w2, rwkv_a0, rwkv_a2, rwkv_k_k, rwkv_k_a,
           rwkv_r_k, ln_x_w, ln_x_b, attn_sink, w_out, final_g):
    depth = norm_g.shape[0]
    layers = []
    for l in range(depth):
        w2_top, pad = _lora_weights(rwkv_w2[l])
        a2_top, _ = _lora_weights(rwkv_a2[l])
        layers.append(dict(
            norm_g=norm_g[l][None, :], w_in=w_in[l].astype(BF16), mu=tshift_mu[l][None, :],
            w0=rwkv_w0[l], w2p=jnp.concatenate([w2_top, pad], axis=0).astype(BF16),
            a0=rwkv_a0[l], a2p=jnp.concatenate([pad, a2_top], axis=0).astype(BF16),
            k_k=rwkv_k_k[l][None, :], k_a=rwkv_k_a[l][None, :], r_k=rwkv_r_k[l].reshape(1, A_W),
            ln_w=ln_x_w[l][None, :], ln_b=ln_x_b[l][None, :], sink=attn_sink[l],
            w_out=w_out[l].astype(BF16)))
    gsum = _head_block_matrix(1.0)
    gavg = _head_block_matrix(1.0 / HEAD_DIM)
    fg = final_g[None, :]
    return (_trunk(x_prompt, layers, fg, gsum, gavg), _trunk(x_sample, layers, fg, gsum, gavg))
```

```python
import functools
import math

import jax
import jax.numpy as jnp
import numpy as np
from jax import lax
from jax.experimental import pallas as pl
from jax.experimental.pallas import tpu as pltpu

F32 = jnp.float32
BF16 = jnp.bfloat16

D_MODEL = 1024
HEAD_DIM = 64
A_HEADS = 4
A_W = A_HEADS * HEAD_DIM
LORA = 64
B_HEADS = 4
B_W = B_HEADS * HEAD_DIM
DILATED_PAIRS = ((128, 1), (512, 4), (2048, 16))
C_HEADS = 8
C_KV_HEADS = 2
C_QW = C_HEADS * HEAD_DIM
C_KVW = C_KV_HEADS * HEAD_DIM
C_RADIUS = 128
ROPE_THETA = 10000.0
RMS_EPS = 1e-5
HEADNORM_EPS = 64e-5
NEG_INF = -1e30
TSHIFT_W = 3 * A_W + 4 * LORA
IN_W = 3584

TOK_R, TOK_V, TOK_KK, TOK_BONUS, TOK_W = 0, A_W, 2 * A_W, 3 * A_W, 4 * A_W
TOK_SCAN_W = 3 * A_W
DIR_LOGW, DIR_KEFF, DIR_BETA, DIR_W = 0, A_W, 2 * A_W, 3 * A_W
GATE_A, GATE_B, GATE_C, GATE_W = 0, A_W, A_W + B_W, A_W + B_W + C_QW
BQKV_W = 3 * B_W
CQKV_W = C_QW + 2 * C_KVW

COL_AG = TSHIFT_W
COL_BQ = COL_AG + A_W
COL_BK = COL_BQ + B_W
COL_BV = COL_BK + B_W
COL_BG = COL_BV + B_W
COL_CQ = COL_BG + B_W
COL_CK = COL_CQ + C_QW
COL_CV = COL_CK + C_KVW
COL_CG = COL_CV + C_KVW

LANES = 128
PROJ_ROWS = 512
POST_ROWS = 1024
DIL_ROWS = 1024
WIN_ROWS = 512
WKV_CHUNK = 64
WKV_ROWS = 512
WKV_SEQS = 2
HALO_ROWS = 8
BAND_GROUP_DIL = 2
BAND_GROUP_WIN = 8
BAND_QBLK = 128
VMEM_LIMIT = 56 * 1024 * 1024


def _dot(a, b, dims):
    return lax.dot_general(a, b, (dims, ((), ())), preferred_element_type=F32)


def _mm(a, b):
    return _dot(a, b, ((1,), (0,)))


def _mm_nt(a, b):
    return _dot(a, b, ((1,), (1,)))


def _bf16_terms(x, n):
    terms = []
    for _ in range(n):
        t = x.astype(BF16)
        terms.append(t)
        x = x - t.astype(F32)
    return terms


def _mm_split_rhs(a_bf16, b, n=3):
    parts = _bf16_terms(b, n)
    out = _mm(a_bf16, parts[-1])
    for t in reversed(parts[:-1]):
        out = out + _mm(a_bf16, t)
    return out


def _sigmoid(x):
    return 1.0 / (1.0 + jnp.exp(-x))


def _silu(x):
    return x * _sigmoid(x)


def _softplus(x):
    return jnp.maximum(x, 0.0) + jnp.log(1.0 + jnp.exp(-jnp.abs(x)))


def _proj_kernel(x_ref, xp_ref, xn_ref, g_ref, w_ref, cos_ref, sin_ref,
                 mu_ref, w0_ref, w2_ref, a0_ref, a2_ref, kk_ref_, ka_ref, rk_ref, gsum_ref,
                 tok_ref, dir_ref, gate_ref, c_ref, b1_ref, b4_ref, b16_ref,
                 tmp_ref):
    i = pl.program_id(1)
    n = pl.num_programs(1)
    tm = x_ref.shape[1]
    g = g_ref[...]

    def normed(x):
        return (x * lax.rsqrt(jnp.mean(x * x, axis=-1, keepdims=True) + RMS_EPS) * g).astype(BF16)

    hb_all = jnp.concatenate([normed(x_ref[0]), normed(xp_ref[0]), normed(xn_ref[0])], axis=0)
    hb = hb_all[:tm]
    prev_at = tm + HALO_ROWS - 1
    next_at = tm + HALO_ROWS
    row = lax.broadcasted_iota(jnp.int32, (tm, 1), 0)

    def shifted(c0, c1):
        full = _mm(hb_all, w_ref[:, c0:c1])
        f = full[:tm]
        prev_row = jnp.where(i > 0, full[prev_at:prev_at + 1], 0.0)
        next_row = jnp.where(i < n - 1, full[next_at:next_at + 1], 0.0)
        prev = jnp.where(row == 0, prev_row, pltpu.roll(f, 1, axis=0))
        nxt = jnp.where(row == tm - 1, next_row, pltpu.roll(f, tm - 1, axis=0))
        return f + mu_ref[:, c0:c1] * (0.5 * (prev + nxt) - f)

    def proj(c0, c1):
        return _mm(hb, w_ref[:, c0:c1])

    r = shifted(0, A_W)
    k = shifted(A_W, 2 * A_W)
    v = shifted(2 * A_W, 3 * A_W)
    lora_in = shifted(3 * A_W, TSHIFT_W)

    cos = cos_ref[...]
    sin = sin_ref[...]
    lane = lax.broadcasted_iota(jnp.int32, (tm, LANES), 1)
    first_half = (lane % HEAD_DIM) < (HEAD_DIM // 2)

    def rope(t):
        outs = []
        for j in range(t.shape[1] // LANES):
            tj = t[:, j * LANES:(j + 1) * LANES]
            partner = jnp.where(first_half,
                                pltpu.roll(tj, LANES - HEAD_DIM // 2, axis=1),
                                pltpu.roll(tj, HEAD_DIM // 2, axis=1))
            outs.append(tj * cos + partner * sin)
        return jnp.concatenate(outs, axis=1) if len(outs) > 1 else outs[0]

    def deinterleave(val, col):
        b1_ref[0, :, col:col + B_W] = val.astype(BF16)
        nl = B_W // LANES
        for j in range(nl):
            tmp_ref[j] = val[:, j * LANES:(j + 1) * LANES]
        for d, ref in ((4, b4_ref), (16, b16_ref)):
            for rr in range(d):
                parts = [tmp_ref[j, pl.ds(rr, tm // d, stride=d), :] for j in range(nl)]
                ref[0, rr, :, col:col + B_W] = jnp.concatenate(parts, axis=1).astype(BF16)

    gate_ref[0, :, GATE_A:GATE_A + A_W] = _silu(proj(COL_AG, COL_AG + A_W)).astype(BF16)
    gate_ref[0, :, GATE_B:GATE_B + B_W] = _silu(proj(COL_BG, COL_BG + B_W)).astype(BF16)
    gate_ref[0, :, GATE_C:GATE_C + C_QW] = _silu(proj(COL_CG, COL_CG + C_QW)).astype(BF16)
    scale = HEAD_DIM ** -0.5
    deinterleave(rope(proj(COL_BQ, COL_BQ + B_W)) * scale, 0)
    deinterleave(rope(proj(COL_BK, COL_BK + B_W)), B_W)
    deinterleave(proj(COL_BV, COL_BV + B_W), 2 * B_W)
    for c0 in range(0, C_QW, 256):
        c_ref[0, :, c0:c0 + 256] = (rope(proj(COL_CQ + c0, COL_CQ + c0 + 256)) * scale).astype(BF16)
    c_ref[0, :, C_QW:C_QW + C_KVW] = rope(proj(COL_CK, COL_CK + C_KVW)).astype(BF16)
    c_ref[0, :, C_QW + C_KVW:C_QW + 2 * C_KVW] = proj(COL_CV, COL_CV + C_KVW).astype(BF16)

    wl = _mm(jnp.tanh(lora_in).astype(BF16), w2_ref[...])
    al = _mm(lora_in.astype(BF16), a2_ref[...])
    gsum = gsum_ref[...]
    kk0 = k * kk_ref_[...]
    ss = _mm((kk0 * kk0).astype(BF16), gsum)
    kkn = kk0 / jnp.maximum(jnp.sqrt(ss), 1e-12)
    tok_ref[0, :, TOK_R:TOK_R + A_W] = r
    tok_ref[0, :, TOK_V:TOK_V + A_W] = v
    tok_ref[0, :, TOK_KK:TOK_KK + A_W] = kkn
    keff_sum = jnp.zeros_like(k)
    for e in range(2):
        w = -_softplus(-(w0_ref[e:e + 1, :] + wl[:, e * A_W:(e + 1) * A_W])) - 0.5
        dir_ref[e, 0, :, DIR_LOGW:DIR_LOGW + A_W] = -jnp.exp(w)
        a = _sigmoid(a0_ref[e:e + 1, :] + al[:, e * A_W:(e + 1) * A_W])
        keff = k * (1.0 + (a - 1.0) * ka_ref[...])
        dir_ref[e, 0, :, DIR_KEFF:DIR_KEFF + A_W] = keff
        dir_ref[e, 0, :, DIR_BETA:DIR_BETA + A_W] = a * kkn
        keff_sum = keff_sum + keff
    tok_ref[0, :, TOK_BONUS:TOK_BONUS + A_W] = _mm((r * keff_sum * rk_ref[...]).astype(BF16), gsum) * v


def _proj_call(x, g, w_bf16, cos_t, sin_t, mu, w0, w2p, a0, a2p, k_k, k_a, r_k, gsum):
    b, t, _ = x.shape
    tm = PROJ_ROWS
    nb = tm // HALO_ROWS
    last = t // HALO_ROWS - 1
    grid = (b, t // tm)

    def row(w):
        return pl.BlockSpec((1, tm, w), lambda bi, i: (bi, i, 0))

    def strided(d):
        return pl.BlockSpec((1, d, tm // d, BQKV_W), lambda bi, i: (bi, 0, i, 0))

    def const(shape):
        return pl.BlockSpec(shape, lambda bi, i: (0,) * len(shape))

    def nat(w, dt):
        return jax.ShapeDtypeStruct((b, t, w), dt)

    def sshape(d):
        return jax.ShapeDtypeStruct((b, d, t // d, BQKV_W), BF16)

    out_shape = [nat(TOK_W, F32), jax.ShapeDtypeStruct((2, b, t, DIR_W), F32), nat(GATE_W, BF16),
                 nat(CQKV_W, BF16), nat(BQKV_W, BF16), sshape(4), sshape(16)]
    out_specs = [row(TOK_W), pl.BlockSpec((2, 1, tm, DIR_W), lambda bi, i: (0, bi, i, 0)), row(GATE_W),
                 row(CQKV_W), row(BQKV_W), strided(4), strided(16)]
    in_specs = [row(D_MODEL),
                pl.BlockSpec((1, HALO_ROWS, D_MODEL), lambda bi, i: (bi, jnp.maximum(i * nb - 1, 0), 0)),
                pl.BlockSpec((1, HALO_ROWS, D_MODEL), lambda bi, i: (bi, jnp.minimum((i + 1) * nb, last), 0)),
                const((1, D_MODEL)), const((D_MODEL, IN_W)),
                pl.BlockSpec((tm, LANES), lambda bi, i: (i, 0)),
                pl.BlockSpec((tm, LANES), lambda bi, i: (i, 0)),
                const((1, TSHIFT_W)), const((2, A_W)), const((4 * LORA, 2 * A_W)),
                const((2, A_W)), const((4 * LORA, 2 * A_W)),
                const((1, A_W)), const((1, A_W)), const((1, A_W)), const((A_W, A_W))]
    return pl.pallas_call(
        _proj_kernel, grid=grid, in_specs=in_specs, out_specs=out_specs, out_shape=out_shape,
        scratch_shapes=[pltpu.VMEM((B_W // LANES, tm, LANES), F32)],
        compiler_params=pltpu.CompilerParams(
            dimension_semantics=("arbitrary", "arbitrary"), vmem_limit_bytes=VMEM_LIMIT),
        name="proj",
    )(x, x, x, g, w_bf16, cos_t, sin_t, mu, w0, w2p, a0, a2p, k_k, k_a, r_k, gsum)


_NN = ((1,), (0,))
_NT = ((1,), (1,))
_TN = ((0,), (0,))
WKV_INV_BASE = 16


def _mmx(a, b, dims, expand=None):
    b = b.astype(BF16)
    return _dot(a.astype(BF16), b if expand is None else expand(b), dims)


def _wkv_kernel(tokf_ref, dirf_ref, tokb_ref, dirb_ref, yf_ref, yb_ref, ht_ref):
    c = WKV_CHUNK
    nchunks = tokf_ref.shape[1] // c
    nh = A_HEADS

    @pl.when(pl.program_id(1) == 0)
    def _():
        ht_ref[...] = jnp.zeros_like(ht_ref)

    row = lax.broadcasted_iota(jnp.int32, (c, c), 0)
    col = lax.broadcasted_iota(jnp.int32, (c, c), 1)
    row_w = lax.broadcasted_iota(jnp.int32, (c, A_W), 0)
    col_w = lax.broadcasted_iota(jnp.int32, (c, A_W), 1) % c
    blk_r = lax.broadcasted_iota(jnp.int32, (A_W, A_W), 0) // HEAD_DIM
    blk_c = lax.broadcasted_iota(jnp.int32, (A_W, A_W), 1) // HEAD_DIM
    same_head = blk_r == blk_c
    eye_w = (row_w == col_w).astype(F32)

    def blockdiag(x):
        return jnp.where(same_head, jnp.concatenate([x] * nh, axis=0), jnp.zeros((), x.dtype))

    def unit_triangular_inverse(a_kb):
        base = WKV_INV_BASE
        same_base = (row_w // base) == (col_w // base)
        p = jnp.where(same_base, -a_kb, 0.0)
        tinv = eye_w + p
        p = _mmx(p, p, _NN, blockdiag)
        yield
        span = 2
        while span < base:
            last = 2 * span >= base
            both = _mmx(tinv if last else jnp.concatenate([tinv, p], axis=0), p, _NN, blockdiag)
            yield
            tinv = tinv + both[:c]
            if not last:
                p = both[c:]
            span *= 2
        size = base
        while size < c:
            coupling = ((row_w // (2 * size)) == (col_w // (2 * size))) & ((row_w // size) != (col_w // size))
            w = _mmx(jnp.where(coupling, a_kb, 0.0), tinv, _NN, blockdiag)
            yield
            tinv = tinv - _mmx(tinv, w, _NN, blockdiag)
            yield
            size *= 2
        return tinv

    def one_chunk(rev, bb, start, tok_ref, dir_ref, y_ref, ht_box, slot):
        rows = pl.ds(start, c)
        incl = (col >= row) if rev else (col <= row)
        incl_w = (col_w >= row_w) if rev else (col_w <= row_w)
        strict_w = (col_w > row_w) if rev else (col_w < row_w)

        lw = dir_ref[0, bb, rows, DIR_LOGW:DIR_LOGW + A_W]
        cum = _mm_split_rhs(incl.astype(BF16), lw)
        g_tot = jnp.exp(jnp.sum(lw, axis=0, keepdims=True))
        g_inv = jnp.exp(-cum)
        v = tok_ref[bb, rows, TOK_V:TOK_V + A_W]
        qt = tok_ref[bb, rows, TOK_R:TOK_R + A_W] * jnp.exp(cum)
        kt = tok_ref[bb, rows, TOK_KK:TOK_KK + A_W] * jnp.exp(cum - lw)
        kb = dir_ref[0, bb, rows, DIR_KEFF:DIR_KEFF + A_W] * g_inv
        bt = dir_ref[0, bb, rows, DIR_BETA:DIR_BETA + A_W] * g_inv
        kq = jnp.concatenate([kt, qt], axis=0)

        a_k = _mmx(kq, kb, _NT, blockdiag)
        yield
        a_b = _mmx(kq, bt, _NT, blockdiag)
        yield
        a_kk = jnp.where(strict_w, a_k[:c], 0.0)
        a_rk = jnp.where(incl_w, a_k[c:], 0.0)
        a_kb = jnp.where(strict_w, a_b[:c], 0.0)
        a_rb = jnp.where(incl_w, a_b[c:], 0.0)
        av = _mmx(jnp.concatenate([a_kk, a_rk], axis=0), v, _NN, blockdiag)
        yield
        tinv = yield from unit_triangular_inverse(a_kb)

        while ht_box[slot] is None:
            yield
        ht = ht_box[slot]
        hq = _mmx(kq, ht, _NN)
        yield
        u = _mmx(tinv, hq[:c] + av[:c], _NN, blockdiag)
        yield
        upd = _mmx(jnp.concatenate([kb * g_tot, -(bt * g_tot)], axis=0), jnp.concatenate([v, u], axis=0), _TN)
        g_col = jnp.broadcast_to(g_tot, (A_W, A_W)).T
        ht_box[slot + 1] = ht * g_col + jnp.where(same_head, upd, 0.0)
        yield
        y_ref[bb, rows, :] = hq[c:] + av[c:] - _mmx(a_rb, u, _NN, blockdiag)

    nseq = tokf_ref.shape[0]
    boxes = [[[ht_ref[e, bb]] + [None] * nchunks for bb in range(nseq)] for e in range(2)]
    tasks = []
    for ci in range(nchunks):
        for bb in range(nseq):
            tasks.append(one_chunk(False, bb, ci * c, tokf_ref, dirf_ref, yf_ref, boxes[0][bb], ci))
            tasks.append(one_chunk(True, bb, (nchunks - 1 - ci) * c, tokb_ref, dirb_ref, yb_ref, boxes[1][bb], ci))
    while tasks:
        alive = []
        for task in tasks:
            try:
                next(task)
                alive.append(task)
            except StopIteration:
                pass
        tasks = alive
    for e in range(2):
        for bb in range(nseq):
            ht_ref[e, bb] = boxes[e][bb][nchunks]


def _wkv_call(tok, dirq):
    b, t, _ = tok.shape
    tb = WKV_ROWS
    nseq = min(WKV_SEQS, b)
    nblk = t // tb
    grid = (b // nseq, nblk)
    tok_f = pl.BlockSpec((nseq, tb, TOK_SCAN_W), lambda bi, j: (bi, j, 0))
    tok_b = pl.BlockSpec((nseq, tb, TOK_SCAN_W), lambda bi, j: (bi, nblk - 1 - j, 0))
    dir_f = pl.BlockSpec((1, nseq, tb, DIR_W), lambda bi, j: (0, bi, j, 0))
    dir_b = pl.BlockSpec((1, nseq, tb, DIR_W), lambda bi, j: (1, bi, nblk - 1 - j, 0))
    y_f = pl.BlockSpec((nseq, tb, A_W), lambda bi, j: (bi, j, 0))
    y_b = pl.BlockSpec((nseq, tb, A_W), lambda bi, j: (bi, nblk - 1 - j, 0))
    out = jax.ShapeDtypeStruct((b, t, A_W), F32)
    return pl.pallas_call(
        _wkv_kernel, grid=grid, in_specs=[tok_f, dir_f, tok_b, dir_b],
        out_specs=[y_f, y_b], out_shape=[out, out],
        scratch_shapes=[pltpu.VMEM((2, nseq, A_W, A_W), F32)],
        compiler_params=pltpu.CompilerParams(
            dimension_semantics=("arbitrary", "arbitrary"), vmem_limit_bytes=VMEM_LIMIT),
        name="wkv",
    )(tok, dirq, tok, dirq)


def _band_kernel(*refs, radius, hq, hkv, seq_len, has_sink, want_lse, qblk, group):
    refs = list(refs)
    sink_ref = refs.pop(0) if has_sink else None
    prev_ref, main_ref, next_ref = refs[:3]
    o_ref = refs[3]
    lse_ref = refs[4] if want_lse else None
    kt_ref, vt_ref, ot_ref, m_ref, d_ref = refs[-5:]
    rad = radius
    tq = main_ref.shape[1]
    grp = hq // hkv
    wq = hq * HEAD_DIM
    wk = hkv * HEAD_DIM
    win = qblk + 2 * rad
    nq = tq // qblk
    i = pl.program_id(1)

    def transposed(c0):
        cat = jnp.concatenate([prev_ref[0, :, c0:c0 + wk], main_ref[0, :, c0:c0 + wk], next_ref[0, :, c0:c0 + wk]],
                              axis=0)
        return cat.astype(F32).T.astype(BF16)

    kt_ref[...] = transposed(wq)
    vt_ref[...] = transposed(wq + wk)

    a = lax.broadcasted_iota(jnp.int32, (qblk, win), 0)
    cc = lax.broadcasted_iota(jnp.int32, (qblk, win), 1)
    band_bias = jnp.where(jnp.abs(cc - rad - a) <= rad, 0.0, NEG_INF)

    def bias_for(j):
        bias = band_bias
        kpos = i * tq + j * qblk - rad + cc
        if j == 0:
            bias = bias + jnp.where(kpos >= 0, 0.0, NEG_INF)
        if j == nq - 1:
            bias = bias + jnp.where(kpos < seq_len, 0.0, NEG_INF)
        return bias

    def scores(j, g, hh, bias):
        h = g * grp + hh
        q = main_ref[0, j * qblk:(j + 1) * qblk, h * HEAD_DIM:(h + 1) * HEAD_DIM]
        kwin = kt_ref[g * HEAD_DIM:(g + 1) * HEAD_DIM, j * qblk:j * qblk + win]
        s = _mm(q, kwin) + bias
        m = jnp.max(s, axis=-1, keepdims=True)
        if has_sink:
            m = jnp.maximum(m, sink_ref[h])
        p = jnp.exp(s - m)
        denom = jnp.sum(p, axis=-1, keepdims=True)
        if has_sink:
            denom = denom + jnp.exp(sink_ref[h] - m)
        return p.astype(BF16), m, denom

    def values(j, g, hh, p, m, denom):
        h = g * grp + hh
        qrows = slice(j * qblk, (j + 1) * qblk)
        hcols = slice(h * HEAD_DIM, (h + 1) * HEAD_DIM)
        vwin = vt_ref[g * HEAD_DIM:(g + 1) * HEAD_DIM, j * qblk:j * qblk + win]
        ot_ref[hcols, qrows] = _mm_nt(vwin, p)
        d_ref[qrows, hcols] = jnp.broadcast_to(denom, (qblk, HEAD_DIM))
        if want_lse:
            m_ref[qrows, hcols] = jnp.broadcast_to(m, (qblk, HEAD_DIM))

    bodies = [(j, g, hh) for j in range(nq) for g in range(hkv) for hh in range(grp)]
    groups = [bodies[n:n + group] for n in range(0, len(bodies), group)]
    biases = {}

    def run_scores(grp_bodies):
        out = []
        for (j, g, hh) in grp_bodies:
            if j not in biases:
                biases[j] = bias_for(j)
            out.append(scores(j, g, hh, biases[j]))
        return out

    pending = run_scores(groups[0])
    for n, grp_bodies in enumerate(groups):
        nxt = run_scores(groups[n + 1]) if n + 1 < len(groups) else None
        for body, state in zip(grp_bodies, pending):
            values(*body, *state)
        pending = nxt

    denom = d_ref[...]
    o_ref[0] = (ot_ref[...].T / denom).astype(o_ref.dtype)
    if want_lse:
        lse_ref[0] = m_ref[...] + jnp.log(denom)


def _band_call(qkv, *, radius, hq, hkv, rows, group, sink=None, want_lse=False, name):
    s, l, w = qkv.shape
    wq = hq * HEAD_DIM
    wk = hkv * HEAD_DIM
    tq = min(rows, l)
    nb = tq // radius
    last = l // radius - 1
    grid = (s, l // tq)
    main = pl.BlockSpec((1, tq, w), lambda si, i: (si, i, 0))
    prev = pl.BlockSpec((1, radius, w), lambda si, i: (si, jnp.maximum(i * nb - 1, 0), 0))
    nxt = pl.BlockSpec((1, radius, w), lambda si, i: (si, jnp.minimum((i + 1) * nb, last), 0))
    out_blk = pl.BlockSpec((1, tq, wq), lambda si, i: (si, i, 0))
    in_specs = [prev, main, nxt]
    args = [qkv, qkv, qkv]
    if sink is not None:
        in_specs = [pl.BlockSpec(memory_space=pltpu.SMEM)] + in_specs
        args = [sink] + args
    out_specs = [out_blk]
    out_shape = [jax.ShapeDtypeStruct((s, l, wq), BF16)]
    if want_lse:
        out_specs.append(out_blk)
        out_shape.append(jax.ShapeDtypeStruct((s, l, wq), F32))
    kern = functools.partial(_band_kernel, radius=radius, hq=hq, hkv=hkv, seq_len=l,
                             has_sink=sink is not None, want_lse=want_lse, qblk=BAND_QBLK, group=group)
    return pl.pallas_call(
        kern, grid=grid, in_specs=in_specs, out_specs=out_specs, out_shape=out_shape,
        scratch_shapes=[pltpu.VMEM((wk, tq + 2 * radius), BF16), pltpu.VMEM((wk, tq + 2 * radius), BF16),
                        pltpu.VMEM((wq, tq), F32), pltpu.VMEM((tq, wq), F32), pltpu.VMEM((tq, wq), F32)],
        compiler_params=pltpu.CompilerParams(
            dimension_semantics=("arbitrary", "arbitrary"), vmem_limit_bytes=VMEM_LIMIT),
        name=name,
    )(*args)


def _post_kernel(*refs, final):
    refs = list(refs)
    (x_ref, yf_ref, yb_ref, bonus_ref, gate_ref, o1_ref, l1_ref, o4_ref, l4_ref, o16_ref, l16_ref,
     co_ref, lnw_ref, lnb_ref, gavg_ref, wo_ref) = refs[:16]
    fg_ref = refs[16] if final else None
    out_ref = refs[-5]
    so4, sl4, so16, sl16 = refs[-4:]
    tm = x_ref.shape[1]

    gavg = gavg_ref[...]
    ya = yf_ref[0] + yb_ref[0]
    mu = _mm(ya.astype(BF16), gavg)
    dev = ya - mu
    var = _mm((dev * dev).astype(BF16), gavg)
    yn = dev * lax.rsqrt(var + HEADNORM_EPS) * lnw_ref[...] + lnb_ref[...]
    mix_a = (yn + bonus_ref[0]) * gate_ref[0, :, GATE_A:GATE_A + A_W].astype(F32)

    nl = B_W // LANES

    def interleave(src_ref, dst, d):
        for r in range(d):
            for j in range(nl):
                dst[j, pl.ds(r, tm // d, stride=d), :] = src_ref[0, r, :, j * LANES:(j + 1) * LANES].astype(F32)
        return jnp.concatenate([dst[j] for j in range(nl)], axis=1)

    o4 = interleave(o4_ref, so4, 4)
    l4 = interleave(l4_ref, sl4, 4)
    o16 = interleave(o16_ref, so16, 16)
    l16 = interleave(l16_ref, sl16, 16)
    l1 = l1_ref[0]
    lmax = jnp.maximum(jnp.maximum(l1, l4), l16)
    w1 = jnp.exp(l1 - lmax)
    w4 = jnp.exp(l4 - lmax)
    w16 = jnp.exp(l16 - lmax)
    mix_b = ((w1 * o1_ref[0].astype(F32) + w4 * o4 + w16 * o16) / (w1 + w4 + w16)
             * gate_ref[0, :, GATE_B:GATE_B + B_W].astype(F32))

    mix_c = co_ref[0].astype(F32) * gate_ref[0, :, GATE_C:GATE_C + C_QW].astype(F32)
    xn = (x_ref[0]
          + _mm(mix_a.astype(BF16), wo_ref[0:A_W, :])
          + _mm(mix_b.astype(BF16), wo_ref[A_W:A_W + B_W, :])
          + _mm(mix_c.astype(BF16), wo_ref[A_W + B_W:, :]))
    if final:
        xn = xn * lax.rsqrt(jnp.mean(xn * xn, axis=-1, keepdims=True) + RMS_EPS) * fg_ref[...]
    out_ref[0] = xn


def _post_call(x, yf, yb, tok, gates, o1, l1, o4, l4, o16, l16, co, lnw, lnb, gavg, wo_bf16, final_g):
    b, t, _ = x.shape
    tm = POST_ROWS
    grid = (b, t // tm)
    final = final_g is not None

    def row(w):
        return pl.BlockSpec((1, tm, w), lambda bi, i: (bi, i, 0))

    def strided(d):
        return pl.BlockSpec((1, d, tm // d, B_W), lambda bi, i: (bi, 0, i, 0))

    def const(shape):
        return pl.BlockSpec(shape, lambda bi, i: (0,) * len(shape))

    bonus_blk = pl.BlockSpec((1, tm, A_W), lambda bi, i: (bi, i, TOK_BONUS // A_W))
    in_specs = [row(D_MODEL), row(A_W), row(A_W), bonus_blk, row(GATE_W),
                row(B_W), row(B_W), strided(4), strided(4), strided(16), strided(16), row(C_QW),
                const((1, A_W)), const((1, A_W)), const((A_W, A_W)), const((D_MODEL, D_MODEL))]
    args = [x, yf, yb, tok, gates, o1, l1, o4, l4, o16, l16, co, lnw, lnb, gavg, wo_bf16]
    if final:
        in_specs.append(const((1, D_MODEL)))
        args.append(final_g)
    return pl.pallas_call(
        functools.partial(_post_kernel, final=final), grid=grid, in_specs=in_specs,
        out_specs=row(D_MODEL), out_shape=jax.ShapeDtypeStruct((b, t, D_MODEL), F32),
        scratch_shapes=[pltpu.VMEM((B_W // LANES, tm, LANES), F32)] * 4,
        compiler_params=pltpu.CompilerParams(
            dimension_semantics=("arbitrary", "arbitrary"), vmem_limit_bytes=VMEM_LIMIT),
        name="post",
    )(*args)


def _rope_tables(t):
    inv = ROPE_THETA ** (-jnp.arange(0, HEAD_DIM, 2, dtype=F32) / HEAD_DIM)
    ang = jnp.arange(t, dtype=F32)[:, None] * inv[None, :]
    cos = jnp.cos(ang)
    sin = jnp.sin(ang)
    reps = LANES // HEAD_DIM
    cos_t = jnp.tile(jnp.concatenate([cos, cos], axis=1), (1, reps))
    sin_t = jnp.tile(jnp.concatenate([-sin, sin], axis=1), (1, reps))
    return cos_t, sin_t


def _head_block_matrix(value):
    idx = np.arange(A_W) // HEAD_DIM
    return jnp.asarray((idx[:, None] == idx[None, :]).astype(np.float32) * value, dtype=BF16)


def _lora_weights(w2):
    z = jnp.zeros((LORA, A_W), F32)
    top = jnp.concatenate([jnp.concatenate([w2[0], z], axis=1), jnp.concatenate([z, w2[1]], axis=1)], axis=0)
    pad = jnp.zeros((2 * LORA, 2 * A_W), F32)
    return top, pad


def _layer(x, p, tables, gsum, gavg, final_g):
    b, t, _ = x.shape
    cos_t, sin_t = tables
    tok, dirq, gates, cqkv, b1, b4, b16 = _proj_call(
        x, p["norm_g"], p["w_in"], cos_t, sin_t,
        p["mu"], p["w0"], p["w2p"], p["a0"], p["a2p"], p["k_k"], p["k_a"], p["r_k"], gsum)
    yf, yb = _wkv_call(tok, dirq)

    branch = []
    for (window, d), qkv in zip(DILATED_PAIRS, (b1, b4, b16)):
        l = t // d
        o, lse = _band_call(qkv.reshape(b * d, l, BQKV_W), radius=window // (2 * d), hq=B_HEADS, hkv=B_HEADS,
                            rows=DIL_ROWS, group=BAND_GROUP_DIL, want_lse=True, name=f"dil{d}")
        branch += [o.reshape(b, d, l, B_W) if d > 1 else o, lse.reshape(b, d, l, B_W) if d > 1 else lse]
    (co,) = _band_call(cqkv, radius=C_RADIUS, hq=C_HEADS, hkv=C_KV_HEADS, rows=WIN_ROWS, group=BAND_GROUP_WIN,
                       sink=p["sink"], name="win")

    return _post_call(x, yf, yb, tok, gates, *branch, co, p["ln_w"], p["ln_b"], gavg, p["w_out"], final_g)


def _trunk(x, layers, final_g, gsum, gavg):
    tables = _rope_tables(x.shape[1])
    for li, p in enumerate(layers):
        x = _layer(x, p, tables, gsum, gavg, final_g if li == len(layers) - 1 else None)
    return x


def kernel(x_prompt, x_sample, norm_g, w_in, tshift_mu, rwkv_w0, rwkv_w2, rwkv_a0, rwkv_a2, rwkv_k_k, rwkv_k_a,
           rwkv_r_k, ln_x_w, ln_x_b, attn_sink, w_out, final_g):
    depth = norm_g.shape[0]
    layers = []
    for l in range(depth):
        w2_top, pad = _lora_weights(rwkv_w2[l])
        a2_top, _ = _lora_weights(rwkv_a2[l])
        layers.append(dict(
            norm_g=norm_g[l][None, :], w_in=w_in[l].astype(BF16), mu=tshift_mu[l][None, :],
            w0=rwkv_w0[l], w2p=jnp.concatenate([w2_top, pad], axis=0).astype(BF16),
            a0=rwkv_a0[l], a2p=jnp.concatenate([pad, a2_top], axis=0).astype(BF16),
            k_k=rwkv_k_k[l][None, :], k_a=rwkv_k_a[l][None, :], r_k=rwkv_r_k[l].reshape(1, A_W),
            ln_w=ln_x_w[l][None, :], ln_b=ln_x_b[l][None, :], sink=attn_sink[l],
            w_out=w_out[l].astype(BF16)))
    gsum = _head_block_matrix(1.0)
    gavg = _head_block_matrix(1.0 / HEAD_DIM)
    fg = final_g[None, :]
    return (_trunk(x_prompt, layers, fg, gsum, gavg), _trunk(x_sample, layers, fg, gsum, gavg))
```

```python
import functools
import math

import jax
import jax.numpy as jnp
import numpy as np
from jax import lax
from jax.experimental import pallas as pl
from jax.experimental.pallas import tpu as pltpu

F32 = jnp.float32
BF16 = jnp.bfloat16

D_MODEL = 1024
HEAD_DIM = 64
A_HEADS = 4
A_W = A_HEADS * HEAD_DIM
LORA = 64
B_HEADS = 4
B_W = B_HEADS * HEAD_DIM
DILATED_PAIRS = ((128, 1), (512, 4), (2048, 16))
C_HEADS = 8
C_KV_HEADS = 2
C_QW = C_HEADS * HEAD_DIM
C_KVW = C_KV_HEADS * HEAD_DIM
C_RADIUS = 128
ROPE_THETA = 10000.0
RMS_EPS = 1e-5
HEADNORM_EPS = 64e-5
NEG_INF = -1e30
TSHIFT_W = 3 * A_W + 4 * LORA
IN_W = 3584

TOK_R, TOK_V, TOK_KK, TOK_BONUS, TOK_W = 0, A_W, 2 * A_W, 3 * A_W, 4 * A_W
TOK_SCAN_W = 3 * A_W
DIR_KEFF, DIR_BETA, DIR_W = 0, A_W, 2 * A_W
GATE_A, GATE_B, GATE_C, GATE_W = 0, A_W, A_W + B_W, A_W + B_W + C_QW
BQKV_W = 3 * B_W
CQKV_W = C_QW + 2 * C_KVW

COL_AG = TSHIFT_W
COL_BQ = COL_AG + A_W
COL_BK = COL_BQ + B_W
COL_BV = COL_BK + B_W
COL_BG = COL_BV + B_W
COL_CQ = COL_BG + B_W
COL_CK = COL_CQ + C_QW
COL_CV = COL_CK + C_KVW
COL_CG = COL_CV + C_KVW

LANES = 128
PROJ_ROWS = 512
POST_ROWS = 1024
DIL_ROWS = 1024
WIN_ROWS = 512
WKV_CHUNK = 64
WKV_ROWS = 512
WKV_SEQS = 2
HALO_ROWS = 8
BAND_GROUP_DIL = 2
BAND_GROUP_WIN = 8
BAND_QBLK = 128
VMEM_LIMIT = 56 * 1024 * 1024


def _dot(a, b, dims):
    return lax.dot_general(a, b, (dims, ((), ())), preferred_element_type=F32)


def _mm(a, b):
    return _dot(a, b, ((1,), (0,)))


def _mm_nt(a, b):
    return _dot(a, b, ((1,), (1,)))


def _bf16_terms(x, n):
    terms = []
    for _ in range(n):
        t = x.astype(BF16)
        terms.append(t)
        x = x - t.astype(F32)
    return terms


def _mm_split_rhs(a_bf16, b, n=3):
    parts = _bf16_terms(b, n)
    out = _mm(a_bf16, parts[-1])
    for t in reversed(parts[:-1]):
        out = out + _mm(a_bf16, t)
    return out


def _sigmoid(x):
    return 1.0 / (1.0 + jnp.exp(-x))


def _silu(x):
    return x * _sigmoid(x)


def _softplus(x):
    return jnp.maximum(x, 0.0) + jnp.log(1.0 + jnp.exp(-jnp.abs(x)))


def _proj_kernel(x_ref, xp_ref, xn_ref, g_ref, w_ref, cos_ref, sin_ref,
                 mu_ref, w0_ref, w2_ref, a0_ref, a2_ref, kk_ref_, ka_ref, rk_ref, gsum_ref,
                 tok_ref, logw_ref, dir_ref, gate_ref, c_ref, b1_ref, b4_ref, b16_ref,
                 tmp_ref):
    i = pl.program_id(1)
    n = pl.num_programs(1)
    tm = x_ref.shape[1]
    g = g_ref[...]

    def normed(x):
        return (x * lax.rsqrt(jnp.mean(x * x, axis=-1, keepdims=True) + RMS_EPS) * g).astype(BF16)

    hb_all = jnp.concatenate([normed(x_ref[0]), normed(xp_ref[0]), normed(xn_ref[0])], axis=0)
    hb = hb_all[:tm]
    prev_at = tm + HALO_ROWS - 1
    next_at = tm + HALO_ROWS
    row = lax.broadcasted_iota(jnp.int32, (tm, 1), 0)

    def shifted(c0, c1):
        full = _mm(hb_all, w_ref[:, c0:c1])
        f = full[:tm]
        prev_row = jnp.where(i > 0, full[prev_at:prev_at + 1], 0.0)
        next_row = jnp.where(i < n - 1, full[next_at:next_at + 1], 0.0)
        prev = jnp.where(row == 0, prev_row, pltpu.roll(f, 1, axis=0))
        nxt = jnp.where(row == tm - 1, next_row, pltpu.roll(f, tm - 1, axis=0))
        return f + mu_ref[:, c0:c1] * (0.5 * (prev + nxt) - f)

    def proj(c0, c1):
        return _mm(hb, w_ref[:, c0:c1])

    r = shifted(0, A_W)
    k = shifted(A_W, 2 * A_W)
    v = shifted(2 * A_W, 3 * A_W)
    lora_in = shifted(3 * A_W, TSHIFT_W)

    cos = cos_ref[...]
    sin = sin_ref[...]
    lane = lax.broadcasted_iota(jnp.int32, (tm, LANES), 1)
    first_half = (lane % HEAD_DIM) < (HEAD_DIM // 2)

    def rope(t):
        outs = []
        for j in range(t.shape[1] // LANES):
            tj = t[:, j * LANES:(j + 1) * LANES]
            partner = jnp.where(first_half,
                                pltpu.roll(tj, LANES - HEAD_DIM // 2, axis=1),
                                pltpu.roll(tj, HEAD_DIM // 2, axis=1))
            outs.append(tj * cos + partner * sin)
        return jnp.concatenate(outs, axis=1) if len(outs) > 1 else outs[0]

    def deinterleave(val, col):
        b1_ref[0, :, col:col + B_W] = val.astype(BF16)
        nl = B_W // LANES
        for j in range(nl):
            tmp_ref[j] = val[:, j * LANES:(j + 1) * LANES]
        for d, ref in ((4, b4_ref), (16, b16_ref)):
            for rr in range(d):
                parts = [tmp_ref[j, pl.ds(rr, tm // d, stride=d), :] for j in range(nl)]
                ref[0, rr, :, col:col + B_W] = jnp.concatenate(parts, axis=1).astype(BF16)

    gate_ref[0, :, GATE_A:GATE_A + A_W] = _silu(proj(COL_AG, COL_AG + A_W)).astype(BF16)
    gate_ref[0, :, GATE_B:GATE_B + B_W] = _silu(proj(COL_BG, COL_BG + B_W)).astype(BF16)
    gate_ref[0, :, GATE_C:GATE_C + C_QW] = _silu(proj(COL_CG, COL_CG + C_QW)).astype(BF16)
    scale = HEAD_DIM ** -0.5
    deinterleave(rope(proj(COL_BQ, COL_BQ + B_W)) * scale, 0)
    deinterleave(rope(proj(COL_BK, COL_BK + B_W)), B_W)
    deinterleave(proj(COL_BV, COL_BV + B_W), 2 * B_W)
    for c0 in range(0, C_QW, 256):
        c_ref[0, :, c0:c0 + 256] = (rope(proj(COL_CQ + c0, COL_CQ + c0 + 256)) * scale).astype(BF16)
    c_ref[0, :, C_QW:C_QW + C_KVW] = rope(proj(COL_CK, COL_CK + C_KVW)).astype(BF16)
    c_ref[0, :, C_QW + C_KVW:C_QW + 2 * C_KVW] = proj(COL_CV, COL_CV + C_KVW).astype(BF16)

    wl = _mm(jnp.tanh(lora_in).astype(BF16), w2_ref[...])
    al = _mm(lora_in.astype(BF16), a2_ref[...])
    gsum = gsum_ref[...]
    kk0 = k * kk_ref_[...]
    ss = _mm((kk0 * kk0).astype(BF16), gsum)
    kkn = kk0 / jnp.maximum(jnp.sqrt(ss), 1e-12)
    tok_ref[0, :, TOK_R:TOK_R + A_W] = r.astype(BF16)
    tok_ref[0, :, TOK_V:TOK_V + A_W] = v.astype(BF16)
    tok_ref[0, :, TOK_KK:TOK_KK + A_W] = kkn.astype(BF16)
    keff_sum = jnp.zeros_like(k)
    for e in range(2):
        w = -_softplus(-(w0_ref[e:e + 1, :] + wl[:, e * A_W:(e + 1) * A_W])) - 0.5
        logw_ref[e, 0] = -jnp.exp(w)
        a = _sigmoid(a0_ref[e:e + 1, :] + al[:, e * A_W:(e + 1) * A_W])
        keff = k * (1.0 + (a - 1.0) * ka_ref[...])
        dir_ref[e, 0, :, DIR_KEFF:DIR_KEFF + A_W] = keff.astype(BF16)
        dir_ref[e, 0, :, DIR_BETA:DIR_BETA + A_W] = (a * kkn).astype(BF16)
        keff_sum = keff_sum + keff
    bonus = _mm((r * keff_sum * rk_ref[...]).astype(BF16), gsum) * v
    tok_ref[0, :, TOK_BONUS:TOK_BONUS + A_W] = bonus.astype(BF16)


def _proj_call(x, g, w_bf16, cos_t, sin_t, mu, w0, w2p, a0, a2p, k_k, k_a, r_k, gsum):
    b, t, _ = x.shape
    tm = PROJ_ROWS
    nb = tm // HALO_ROWS
    last = t // HALO_ROWS - 1
    grid = (b, t // tm)

    def row(w):
        return pl.BlockSpec((1, tm, w), lambda bi, i: (bi, i, 0))

    def strided(d):
        return pl.BlockSpec((1, d, tm // d, BQKV_W), lambda bi, i: (bi, 0, i, 0))

    def const(shape):
        return pl.BlockSpec(shape, lambda bi, i: (0,) * len(shape))

    def nat(w, dt):
        return jax.ShapeDtypeStruct((b, t, w), dt)

    def sshape(d):
        return jax.ShapeDtypeStruct((b, d, t // d, BQKV_W), BF16)

    out_shape = [nat(TOK_W, BF16), jax.ShapeDtypeStruct((2, b, t, A_W), F32),
                 jax.ShapeDtypeStruct((2, b, t, DIR_W), BF16), nat(GATE_W, BF16),
                 nat(CQKV_W, BF16), nat(BQKV_W, BF16), sshape(4), sshape(16)]
    out_specs = [row(TOK_W), pl.BlockSpec((2, 1, tm, A_W), lambda bi, i: (0, bi, i, 0)),
                 pl.BlockSpec((2, 1, tm, DIR_W), lambda bi, i: (0, bi, i, 0)), row(GATE_W),
                 row(CQKV_W), row(BQKV_W), strided(4), strided(16)]
    in_specs = [row(D_MODEL),
                pl.BlockSpec((1, HALO_ROWS, D_MODEL), lambda bi, i: (bi, jnp.maximum(i * nb - 1, 0), 0)),
                pl.BlockSpec((1, HALO_ROWS, D_MODEL), lambda bi, i: (bi, jnp.minimum((i + 1) * nb, last), 0)),
                const((1, D_MODEL)), const((D_MODEL, IN_W)),
                pl.BlockSpec((tm, LANES), lambda bi, i: (i, 0)),
                pl.BlockSpec((tm, LANES), lambda bi, i: (i, 0)),
                const((1, TSHIFT_W)), const((2, A_W)), const((4 * LORA, 2 * A_W)),
                const((2, A_W)), const((4 * LORA, 2 * A_W)),
                const((1, A_W)), const((1, A_W)), const((1, A_W)), const((A_W, A_W))]
    return pl.pallas_call(
        _proj_kernel, grid=grid, in_specs=in_specs, out_specs=out_specs, out_shape=out_shape,
        scratch_shapes=[pltpu.VMEM((B_W // LANES, tm, LANES), F32)],
        compiler_params=pltpu.CompilerParams(
            dimension_semantics=("arbitrary", "arbitrary"), vmem_limit_bytes=VMEM_LIMIT),
        name="proj",
    )(x, x, x, g, w_bf16, cos_t, sin_t, mu, w0, w2p, a0, a2p, k_k, k_a, r_k, gsum)


_NN = ((1,), (0,))
_NT = ((1,), (1,))
_TN = ((0,), (0,))
WKV_INV_BASE = 16


def _mmx(a, b, dims, expand=None):
    b = b.astype(BF16)
    return _dot(a.astype(BF16), b if expand is None else expand(b), dims)


def _wkv_kernel(tokf_ref, lwf_ref, dirf_ref, tokb_ref, lwb_ref, dirb_ref, yf_ref, yb_ref, ht_ref):
    c = WKV_CHUNK
    nchunks = tokf_ref.shape[1] // c
    nh = A_HEADS

    @pl.when(pl.program_id(1) == 0)
    def _():
        ht_ref[...] = jnp.zeros_like(ht_ref)

    row = lax.broadcasted_iota(jnp.int32, (c, c), 0)
    col = lax.broadcasted_iota(jnp.int32, (c, c), 1)
    row_w = lax.broadcasted_iota(jnp.int32, (c, A_W), 0)
    col_w = lax.broadcasted_iota(jnp.int32, (c, A_W), 1) % c
    blk_r = lax.broadcasted_iota(jnp.int32, (A_W, A_W), 0) // HEAD_DIM
    blk_c = lax.broadcasted_iota(jnp.int32, (A_W, A_W), 1) // HEAD_DIM
    same_head = blk_r == blk_c
    eye_w = (row_w == col_w).astype(F32)

    def blockdiag(x):
        return jnp.where(same_head, jnp.concatenate([x] * nh, axis=0), jnp.zeros((), x.dtype))

    def unit_triangular_inverse(a_kb):
        base = WKV_INV_BASE
        same_base = (row_w // base) == (col_w // base)
        p = jnp.where(same_base, -a_kb, 0.0)
        tinv = eye_w + p
        p = _mmx(p, p, _NN, blockdiag)
        yield
        span = 2
        while span < base:
            last = 2 * span >= base
            both = _mmx(tinv if last else jnp.concatenate([tinv, p], axis=0), p, _NN, blockdiag)
            yield
            tinv = tinv + both[:c]
            if not last:
                p = both[c:]
            span *= 2
        size = base
        while size < c:
            coupling = ((row_w // (2 * size)) == (col_w // (2 * size))) & ((row_w // size) != (col_w // size))
            w = _mmx(jnp.where(coupling, a_kb, 0.0), tinv, _NN, blockdiag)
            yield
            tinv = tinv - _mmx(tinv, w, _NN, blockdiag)
            yield
            size *= 2
        return tinv

    def one_chunk(rev, bb, start, tok_ref, lw_ref, dir_ref, y_ref, ht_box, slot):
        rows = pl.ds(start, c)
        incl = (col >= row) if rev else (col <= row)
        incl_w = (col_w >= row_w) if rev else (col_w <= row_w)
        strict_w = (col_w > row_w) if rev else (col_w < row_w)

        lw = lw_ref[0, bb, rows, :]
        cum = _mm_split_rhs(incl.astype(BF16), lw)
        g_tot = jnp.exp(jnp.sum(lw, axis=0, keepdims=True))
        g_inv = jnp.exp(-cum)
        v = tok_ref[bb, rows, TOK_V:TOK_V + A_W].astype(F32)
        qt = tok_ref[bb, rows, TOK_R:TOK_R + A_W].astype(F32) * jnp.exp(cum)
        kt = tok_ref[bb, rows, TOK_KK:TOK_KK + A_W].astype(F32) * jnp.exp(cum - lw)
        kb = dir_ref[0, bb, rows, DIR_KEFF:DIR_KEFF + A_W].astype(F32) * g_inv
        bt = dir_ref[0, bb, rows, DIR_BETA:DIR_BETA + A_W].astype(F32) * g_inv
        kq = jnp.concatenate([kt, qt], axis=0)

        a_k = _mmx(kq, kb, _NT, blockdiag)
        yield
        a_b = _mmx(kq, bt, _NT, blockdiag)
        yield
        a_kk = jnp.where(strict_w, a_k[:c], 0.0)
        a_rk = jnp.where(incl_w, a_k[c:], 0.0)
        a_kb = jnp.where(strict_w, a_b[:c], 0.0)
        a_rb = jnp.where(incl_w, a_b[c:], 0.0)
        av = _mmx(jnp.concatenate([a_kk, a_rk], axis=0), v, _NN, blockdiag)
        yield
        tinv = yield from unit_triangular_inverse(a_kb)

        while ht_box[slot] is None:
            yield
        ht = ht_box[slot]
        hq = _mmx(kq, ht, _NN)
        yield
        u = _mmx(tinv, hq[:c] + av[:c], _NN, blockdiag)
        yield
        upd = _mmx(jnp.concatenate([kb * g_tot, -(bt * g_tot)], axis=0), jnp.concatenate([v, u], axis=0), _TN)
        g_col = jnp.broadcast_to(g_tot, (A_W, A_W)).T
        ht_box[slot + 1] = ht * g_col + jnp.where(same_head, upd, 0.0)
        yield
        y_ref[bb, rows, :] = (hq[c:] + av[c:] - _mmx(a_rb, u, _NN, blockdiag)).astype(y_ref.dtype)

    nseq = tokf_ref.shape[0]
    boxes = [[[ht_ref[e, bb]] + [None] * nchunks for bb in range(nseq)] for e in range(2)]
    tasks = []
    for ci in range(nchunks):
        for bb in range(nseq):
            tasks.append(one_chunk(False, bb, ci * c, tokf_ref, lwf_ref, dirf_ref, yf_ref, boxes[0][bb], ci))
            tasks.append(one_chunk(True, bb, (nchunks - 1 - ci) * c, tokb_ref, lwb_ref, dirb_ref, yb_ref,
                                   boxes[1][bb], ci))
    while tasks:
        alive = []
        for task in tasks:
            try:
                next(task)
                alive.append(task)
            except StopIteration:
                pass
        tasks = alive
    for e in range(2):
        for bb in range(nseq):
            ht_ref[e, bb] = boxes[e][bb][nchunks]


def _wkv_call(tok, logw, dirq):
    b, t, _ = tok.shape
    tb = WKV_ROWS
    nseq = min(WKV_SEQS, b)
    nblk = t // tb
    grid = (b // nseq, nblk)

    def fwd(j):
        return j

    def bwd(j):
        return nblk - 1 - j

    def specs(blk):
        return [pl.BlockSpec((nseq, tb, TOK_SCAN_W), lambda bi, j: (bi, blk(j), 0)),
                pl.BlockSpec((1, nseq, tb, A_W), lambda bi, j: (1 if blk is bwd else 0, bi, blk(j), 0)),
                pl.BlockSpec((1, nseq, tb, DIR_W), lambda bi, j: (1 if blk is bwd else 0, bi, blk(j), 0))]

    y_f = pl.BlockSpec((nseq, tb, A_W), lambda bi, j: (bi, fwd(j), 0))
    y_b = pl.BlockSpec((nseq, tb, A_W), lambda bi, j: (bi, bwd(j), 0))
    out = jax.ShapeDtypeStruct((b, t, A_W), BF16)
    return pl.pallas_call(
        _wkv_kernel, grid=grid, in_specs=specs(fwd) + specs(bwd),
        out_specs=[y_f, y_b], out_shape=[out, out],
        scratch_shapes=[pltpu.VMEM((2, nseq, A_W, A_W), F32)],
        compiler_params=pltpu.CompilerParams(
            dimension_semantics=("arbitrary", "arbitrary"), vmem_limit_bytes=VMEM_LIMIT),
        name="wkv",
    )(tok, logw, dirq, tok, logw, dirq)


def _band_kernel(*refs, radius, hq, hkv, seq_len, has_sink, want_lse, qblk, group):
    refs = list(refs)
    sink_ref = refs.pop(0) if has_sink else None
    prev_ref, main_ref, next_ref = refs[:3]
    o_ref = refs[3]
    lse_ref = refs[4] if want_lse else None
    kt_ref, vt_ref, ot_ref, m_ref, d_ref = refs[-5:]
    rad = radius
    tq = main_ref.shape[1]
    grp = hq // hkv
    wq = hq * HEAD_DIM
    wk = hkv * HEAD_DIM
    win = qblk + 2 * rad
    nq = tq // qblk
    i = pl.program_id(1)

    def transposed(c0):
        cat = jnp.concatenate([prev_ref[0, :, c0:c0 + wk], main_ref[0, :, c0:c0 + wk], next_ref[0, :, c0:c0 + wk]],
                              axis=0)
        return cat.astype(F32).T.astype(BF16)

    kt_ref[...] = transposed(wq)
    vt_ref[...] = transposed(wq + wk)

    a = lax.broadcasted_iota(jnp.int32, (qblk, win), 0)
    cc = lax.broadcasted_iota(jnp.int32, (qblk, win), 1)
    band_bias = jnp.where(jnp.abs(cc - rad - a) <= rad, 0.0, NEG_INF)

    def bias_for(j):
        bias = band_bias
        kpos = i * tq + j * qblk - rad + cc
        if j == 0:
            bias = bias + jnp.where(kpos >= 0, 0.0, NEG_INF)
        if j == nq - 1:
            bias = bias + jnp.where(kpos < seq_len, 0.0, NEG_INF)
        return bias

    def scores(j, g, hh, bias):
        h = g * grp + hh
        q = main_ref[0, j * qblk:(j + 1) * qblk, h * HEAD_DIM:(h + 1) * HEAD_DIM]
        kwin = kt_ref[g * HEAD_DIM:(g + 1) * HEAD_DIM, j * qblk:j * qblk + win]
        s = _mm(q, kwin) + bias
        m = jnp.max(s, axis=-1, keepdims=True)
        if has_sink:
            m = jnp.maximum(m, sink_ref[h])
        p = jnp.exp(s - m)
        denom = jnp.sum(p, axis=-1, keepdims=True)
        if has_sink:
            denom = denom + jnp.exp(sink_ref[h] - m)
        return p.astype(BF16), m, denom

    def values(j, g, hh, p, m, denom):
        h = g * grp + hh
        qrows = slice(j * qblk, (j + 1) * qblk)
        hcols = slice(h * HEAD_DIM, (h + 1) * HEAD_DIM)
        vwin = vt_ref[g * HEAD_DIM:(g + 1) * HEAD_DIM, j * qblk:j * qblk + win]
        ot_ref[hcols, qrows] = _mm_nt(vwin, p)
        d_ref[qrows, hcols] = jnp.broadcast_to(denom, (qblk, HEAD_DIM))
        if want_lse:
            m_ref[qrows, hcols] = jnp.broadcast_to(m, (qblk, HEAD_DIM))

    bodies = [(j, g, hh) for j in range(nq) for g in range(hkv) for hh in range(grp)]
    groups = [bodies[n:n + group] for n in range(0, len(bodies), group)]
    biases = {}

    def run_scores(grp_bodies):
        out = []
        for (j, g, hh) in grp_bodies:
            if j not in biases:
                biases[j] = bias_for(j)
            out.append(scores(j, g, hh, biases[j]))
        return out

    pending = run_scores(groups[0])
    for n, grp_bodies in enumerate(groups):
        nxt = run_scores(groups[n + 1]) if n + 1 < len(groups) else None
        for body, state in zip(grp_bodies, pending):
            values(*body, *state)
        pending = nxt

    denom = d_ref[...]
    o_ref[0] = (ot_ref[...].T / denom).astype(o_ref.dtype)
    if want_lse:
        lse_ref[0] = m_ref[...] + jnp.log(denom)


def _band_call(qkv, *, radius, hq, hkv, rows, group, sink=None, want_lse=False, name):
    s, l, w = qkv.shape
    wq = hq * HEAD_DIM
    wk = hkv * HEAD_DIM
    tq = min(rows, l)
    nb = tq // radius
    last = l // radius - 1
    grid = (s, l // tq)
    main = pl.BlockSpec((1, tq, w), lambda si, i: (si, i, 0))
    prev = pl.BlockSpec((1, radius, w), lambda si, i: (si, jnp.maximum(i * nb - 1, 0), 0))
    nxt = pl.BlockSpec((1, radius, w), lambda si, i: (si, jnp.minimum((i + 1) * nb, last), 0))
    out_blk = pl.BlockSpec((1, tq, wq), lambda si, i: (si, i, 0))
    in_specs = [prev, main, nxt]
    args = [qkv, qkv, qkv]
    if sink is not None:
        in_specs = [pl.BlockSpec(memory_space=pltpu.SMEM)] + in_specs
        args = [sink] + args
    out_specs = [out_blk]
    out_shape = [jax.ShapeDtypeStruct((s, l, wq), BF16)]
    if want_lse:
        out_specs.append(out_blk)
        out_shape.append(jax.ShapeDtypeStruct((s, l, wq), F32))
    kern = functools.partial(_band_kernel, radius=radius, hq=hq, hkv=hkv, seq_len=l,
                             has_sink=sink is not None, want_lse=want_lse, qblk=BAND_QBLK, group=group)
    return pl.pallas_call(
        kern, grid=grid, in_specs=in_specs, out_specs=out_specs, out_shape=out_shape,
        scratch_shapes=[pltpu.VMEM((wk, tq + 2 * radius), BF16), pltpu.VMEM((wk, tq + 2 * radius), BF16),
                        pltpu.VMEM((wq, tq), F32), pltpu.VMEM((tq, wq), F32), pltpu.VMEM((tq, wq), F32)],
        compiler_params=pltpu.CompilerParams(
            dimension_semantics=("arbitrary", "arbitrary"), vmem_limit_bytes=VMEM_LIMIT),
        name=name,
    )(*args)


def _post_kernel(*refs, final):
    refs = list(refs)
    (x_ref, yf_ref, yb_ref, bonus_ref, gate_ref, o1_ref, l1_ref, o4_ref, l4_ref, o16_ref, l16_ref,
     co_ref, lnw_ref, lnb_ref, gavg_ref, wo_ref) = refs[:16]
    fg_ref = refs[16] if final else None
    out_ref = refs[-5]
    so4, sl4, so16, sl16 = refs[-4:]
    tm = x_ref.shape[1]

    gavg = gavg_ref[...]
    ya = yf_ref[0].astype(F32) + yb_ref[0].astype(F32)
    mu = _mm(ya.astype(BF16), gavg)
    dev = ya - mu
    var = _mm((dev * dev).astype(BF16), gavg)
    yn = dev * lax.rsqrt(var + HEADNORM_EPS) * lnw_ref[...] + lnb_ref[...]
    mix_a = (yn + bonus_ref[0].astype(F32)) * gate_ref[0, :, GATE_A:GATE_A + A_W].astype(F32)

    nl = B_W // LANES

    def interleave(src_ref, dst, d):
        for r in range(d):
            for j in range(nl):
                dst[j, pl.ds(r, tm // d, stride=d), :] = src_ref[0, r, :, j * LANES:(j + 1) * LANES].astype(F32)
        return jnp.concatenate([dst[j] for j in range(nl)], axis=1)

    o4 = interleave(o4_ref, so4, 4)
    l4 = interleave(l4_ref, sl4, 4)
    o16 = interleave(o16_ref, so16, 16)
    l16 = interleave(l16_ref, sl16, 16)
    l1 = l1_ref[0]
    lmax = jnp.maximum(jnp.maximum(l1, l4), l16)
    w1 = jnp.exp(l1 - lmax)
    w4 = jnp.exp(l4 - lmax)
    w16 = jnp.exp(l16 - lmax)
    mix_b = ((w1 * o1_ref[0].astype(F32) + w4 * o4 + w16 * o16) / (w1 + w4 + w16)
             * gate_ref[0, :, GATE_B:GATE_B + B_W].astype(F32))

    mix_c = co_ref[0].astype(F32) * gate_ref[0, :, GATE_C:GATE_C + C_QW].astype(F32)
    xn = (x_ref[0]
          + _mm(mix_a.astype(BF16), wo_ref[0:A_W, :])
          + _mm(mix_b.astype(BF16), wo_ref[A_W:A_W + B_W, :])
          + _mm(mix_c.astype(BF16), wo_ref[A_W + B_W:, :]))
    if final:
        xn = xn * lax.rsqrt(jnp.mean(xn * xn, axis=-1, keepdims=True) + RMS_EPS) * fg_ref[...]
    out_ref[0] = xn


def _post_call(x, yf, yb, tok, gates, o1, l1, o4, l4, o16, l16, co, lnw, lnb, gavg, wo_bf16, final_g):
    b, t, _ = x.shape
    tm = POST_ROWS
    grid = (b, t // tm)
    final = final_g is not None

    def row(w):
        return pl.BlockSpec((1, tm, w), lambda bi, i: (bi, i, 0))

    def strided(d):
        return pl.BlockSpec((1, d, tm // d, B_W), lambda bi, i: (bi, 0, i, 0))

    def const(shape):
        return pl.BlockSpec(shape, lambda bi, i: (0,) * len(shape))

    bonus_blk = pl.BlockSpec((1, tm, A_W), lambda bi, i: (bi, i, TOK_BONUS // A_W))
    in_specs = [row(D_MODEL), row(A_W), row(A_W), bonus_blk, row(GATE_W),
                row(B_W), row(B_W), strided(4), strided(4), strided(16), strided(16), row(C_QW),
                const((1, A_W)), const((1, A_W)), const((A_W, A_W)), const((D_MODEL, D_MODEL))]
    args = [x, yf, yb, tok, gates, o1, l1, o4, l4, o16, l16, co, lnw, lnb, gavg, wo_bf16]
    if final:
        in_specs.append(const((1, D_MODEL)))
        args.append(final_g)
    return pl.pallas_call(
        functools.partial(_post_kernel, final=final), grid=grid, in_specs=in_specs,
        out_specs=row(D_MODEL), out_shape=jax.ShapeDtypeStruct((b, t, D_MODEL), F32),
        scratch_shapes=[pltpu.VMEM((B_W // LANES, tm, LANES), F32)] * 4,
        compiler_params=pltpu.CompilerParams(
            dimension_semantics=("arbitrary", "arbitrary"), vmem_limit_bytes=VMEM_LIMIT),
        name="post",
    )(*args)


def _rope_tables(t):
    inv = ROPE_THETA ** (-jnp.arange(0, HEAD_DIM, 2, dtype=F32) / HEAD_DIM)
    ang = jnp.arange(t, dtype=F32)[:, None] * inv[None, :]
    cos = jnp.cos(ang)
    sin = jnp.sin(ang)
    reps = LANES // HEAD_DIM
    cos_t = jnp.tile(jnp.concatenate([cos, cos], axis=1), (1, reps))
    sin_t = jnp.tile(jnp.concatenate([-sin, sin], axis=1), (1, reps))
    return cos_t, sin_t


def _head_block_matrix(value):
    idx = np.arange(A_W) // HEAD_DIM
    return jnp.asarray((idx[:, None] == idx[None, :]).astype(np.float32) * value, dtype=BF16)


def _lora_weights(w2):
    z = jnp.zeros((LORA, A_W), F32)
    top = jnp.concatenate([jnp.concatenate([w2[0], z], axis=1), jnp.concatenate([z, w2[1]], axis=1)], axis=0)
    pad = jnp.zeros((2 * LORA, 2 * A_W), F32)
    return top, pad


def _layer(x, p, tables, gsum, gavg, final_g):
    b, t, _ = x.shape
    cos_t, sin_t = tables
    tok, logw, dirq, gates, cqkv, b1, b4, b16 = _proj_call(
        x, p["norm_g"], p["w_in"], cos_t, sin_t,
        p["mu"], p["w0"], p["w2p"], p["a0"], p["a2p"], p["k_k"], p["k_a"], p["r_k"], gsum)
    yf, yb = _wkv_call(tok, logw, dirq)

    branch = []
    for (window, d), qkv in zip(DILATED_PAIRS, (b1, b4, b16)):
        l = t // d
        o, lse = _band_call(qkv.reshape(b * d, l, BQKV_W), radius=window // (2 * d), hq=B_HEADS, hkv=B_HEADS,
                            rows=DIL_ROWS, group=BAND_GROUP_DIL, want_lse=True, name=f"dil{d}")
        branch += [o.reshape(b, d, l, B_W) if d > 1 else o, lse.reshape(b, d, l, B_W) if d > 1 else lse]
    (co,) = _band_call(cqkv, radius=C_RADIUS, hq=C_HEADS, hkv=C_KV_HEADS, rows=WIN_ROWS, group=BAND_GROUP_WIN,
                       sink=p["sink"], name="win")

    return _post_call(x, yf, yb, tok, gates, *branch, co, p["ln_w"], p["ln_b"], gavg, p["w_out"], final_g)


def _trunk(x, layers, final_g, gsum, gavg):
    tables = _rope_tables(x.shape[1])
    for li, p in enumerate(layers):
        x = _layer(x, p, tables, gsum, gavg, final_g if li == len(layers) - 1 else None)
    return x


def kernel(x_prompt, x_sample, norm_g, w_in, tshift_mu, rwkv_w0, rwkv_w2, rwkv_a0, rwkv_a2, rwkv_k_k, rwkv_k_a,
           rwkv_r_k, ln_x_w, ln_x_b, attn_sink, w_out, final_g):
    depth = norm_g.shape[0]
    layers = []
    for l in range(depth):
        w2_top, pad = _lora_weights(rwkv_w2[l])
        a2_top, _ = _lora_weights(rwkv_a2[l])
        layers.append(dict(
            norm_g=norm_g[l][None, :], w_in=w_in[l].astype(BF16), mu=tshift_mu[l][None, :],
            w0=rwkv_w0[l], w2p=jnp.concatenate([w2_top, pad], axis=0).astype(BF16),
            a0=rwkv_a0[l], a2p=jnp.concatenate([pad, a2_top], axis=0).astype(BF16),
            k_k=rwkv_k_k[l][None, :], k_a=rwkv_k_a[l][None, :], r_k=rwkv_r_k[l].reshape(1, A_W),
            ln_w=ln_x_w[l][None, :], ln_b=ln_x_b[l][None, :], sink=attn_sink[l],
            w_out=w_out[l].astype(BF16)))
    gsum = _head_block_matrix(1.0)
    gavg = _head_block_matrix(1.0 / HEAD_DIM)
    fg = final_g[None, :]
    return (_trunk(x_prompt, layers, fg, gsum, gavg), _trunk(x_sample, layers, fg, gsum, gavg))
```
